```python
import jax, jax.numpy as jnp
from jax import lax
import numpy as np

D_MODEL = 1024
BATCH = 8
SEQ = 4096
DEPTH = 2

N_MIXERS = 2
D_TOK = 3 * D_MODEL // 4
D_MEM = D_MODEL - D_TOK
HG_EXPAND = 128
HG_HEADS = D_TOK // HG_EXPAND
HG_VDIM = D_TOK // HG_HEADS
HG_CHUNK = 64
GM_CHUNK = 128
GM_GROUPS = 6
GM_GDIM = D_TOK // GM_GROUPS
MEM_LEN = 256
MEM_HEADS = 4
MEM_HDIM = D_MEM // MEM_HEADS
D_FF = -(-8 * D_MODEL // (3 * 256)) * 256
N_A = (DEPTH + 1) // 2
N_B = DEPTH // 2
EPS = 1e-6

kernel_name = "hybrid_hgrn2_gmlp_memxattn"


def rmsnorm(x, g):
    xf = x.astype(jnp.float32)
    y = xf * lax.rsqrt(jnp.mean(xf * xf, axis=-1, keepdims=True) + EPS)
    return (y * g.astype(jnp.float32)).astype(x.dtype)


def hgrn2_mix(p, lb):
    B, S, _ = p.shape
    n = S // HG_CHUNK
    q, fz, iv, g = jnp.split(p.astype(jnp.float32), 4, axis=-1)
    lbf = lb.astype(jnp.float32)
    log_f = jnp.log(lbf + (1.0 - lbf) * jax.nn.sigmoid(fz))
    k = -jnp.expm1(log_f)

    def to_chunks(t, d):
        return t.reshape(B, n, HG_CHUNK, HG_HEADS, d).transpose(1, 0, 3, 2, 4)

    qc = to_chunks(q, HG_EXPAND)
    kc = to_chunks(k, HG_EXPAND)
    lc = to_chunks(log_f, HG_EXPAND)
    vc = to_chunks(iv, HG_VDIM)
    causal = jnp.tril(jnp.ones((HG_CHUNK, HG_CHUNK), dtype=bool))[None, None, :, :, None]

    def body(state, inp):
        qb, kb, vb, lg = inp
        b = jnp.cumsum(lg, axis=2)
        inter = jnp.einsum('bhtk,bhkv->bhtv', qb * jnp.exp(b), state)
        diff = b[:, :, :, None, :] - b[:, :, None, :, :]
        decay = jnp.where(causal, jnp.exp(jnp.minimum(diff, 0.0)), 0.0)
        scores = jnp.einsum('bhtk,bhsk,bhtsk->bhts', qb, kb, decay)
        intra = jnp.einsum('bhts,bhsv->bhtv', scores, vb)
        b_last = b[:, :, -1:, :]
        new_state = (jnp.exp(b_last[:, :, 0, :])[..., None] * state
                     + jnp.einsum('bhsk,bhsv->bhkv', kb * jnp.exp(b_last - b), vb))
        return new_state, inter + intra

    s0 = jnp.zeros((B, HG_HEADS, HG_EXPAND, HG_VDIM), jnp.float32)
    _, o = lax.scan(body, s0, (qc, kc, vc, lc))
    o = o.transpose(1, 0, 3, 2, 4).reshape(B, S, HG_HEADS, HG_VDIM)
    o = o * lax.rsqrt(jnp.mean(o * o, axis=-1, keepdims=True) + EPS)
    return o.reshape(B, S, D_TOK) * jax.nn.silu(g)


def gmlp_mix(p, ln_g, ln_b, ws, bs):
    B, S, _ = p.shape
    n = S // GM_CHUNK
    z = jax.nn.gelu(p.astype(jnp.float32), approximate=False)
    u, v = jnp.split(z, 2, axis=-1)
    mu = jnp.mean(v, axis=-1, keepdims=True)
    var = jnp.mean(jnp.square(v - mu), axis=-1, keepdims=True)
    v = (v - mu) * lax.rsqrt(var + EPS) * ln_g.astype(jnp.float32) + ln_b.astype(jnp.float32)
    v = v.reshape(B, n, GM_CHUNK, GM_GROUPS, GM_GDIM)
    w = ws.astype(jnp.float32) * jnp.tril(jnp.ones((GM_CHUNK, GM_CHUNK), jnp.float32))[None]
    sv = jnp.einsum('gts,bnsgc->bntgc', w, v) + bs.astype(jnp.float32).T[None, None, :, :, None]
    return u * sv.reshape(B, S, D_TOK)


def mem_attn(qm, mem, g, w_kv):
    B, S, _ = qm.shape
    m = rmsnorm(mem, g)
    kv = m @ w_kv
    k, v = jnp.split(kv, 2, axis=-1)
    k = k.reshape(B, MEM_LEN, MEM_HEADS, MEM_HDIM)
    v = v.reshape(B, MEM_LEN, MEM_HEADS, MEM_HDIM)
    q = qm.reshape(B, S, MEM_HEADS, MEM_HDIM)
    s = jnp.einsum('bshd,bmhd->bhsm', q, k).astype(jnp.float32) * (MEM_HDIM ** -0.5)
    pr = jax.nn.softmax(s, axis=-1)
    o = jnp.einsum('bhsm,bmhd->bshd', pr, v.astype(jnp.float32))
    return o.reshape(B, S, D_MEM)


def swiglu(h, w_in, w_out):
    a = h @ w_in
    gate, up = jnp.split(a, 2, axis=-1)
    return (jax.nn.silu(gate) * up) @ w_out


def setup_inputs(seed: int = 0) -> dict:
    key = jax.random.key(seed)
    ks = jax.random.split(key, 20)
    f32 = jnp.float32

    def nrm(k, shape, s):
        return jax.random.normal(k, shape, f32) * s

    return {
        "x": nrm(ks[0], (BATCH, SEQ, D_MODEL), 1.0),
        "mem": nrm(ks[1], (BATCH, MEM_LEN, D_MODEL), 1.0),
        "mix_norm": 1.0 + nrm(ks[2], (DEPTH, D_MODEL), 0.02),
        "mem_norm": 1.0 + nrm(ks[3], (DEPTH, D_MODEL), 0.02),
        "w_mem_kv": nrm(ks[4], (DEPTH, D_MODEL, 2 * D_MEM), D_MODEL ** -0.5),
        "w_out": nrm(ks[5], (DEPTH, D_TOK + D_MEM, D_MODEL), (D_TOK + D_MEM) ** -0.5),
        "hg_w_in": nrm(ks[6], (N_A, D_MODEL, 4 * D_TOK + D_MEM), D_MODEL ** -0.5),
        "hg_lb": nrm(ks[7], (DEPTH + 1, D_TOK), 0.5),
        "hg_onorm": 1.0 + nrm(ks[8], (N_A, D_TOK), 0.02),
        "gm_w_in": nrm(ks[9], (N_B, D_MODEL, 2 * D_TOK + D_MEM), D_MODEL ** -0.5),
        "gm_ln_g": 1.0 + nrm(ks[10], (N_B, D_TOK), 0.02),
        "gm_ln_b": nrm(ks[11], (N_B, D_TOK), 0.02),
        "gm_ws": nrm(ks[12], (N_B, GM_GROUPS, GM_CHUNK, GM_CHUNK), GM_CHUNK ** -0.5),
        "gm_bs": 1.0 + nrm(ks[13], (N_B, GM_GROUPS, GM_CHUNK), 0.02),
        "ffn_norm": 1.0 + nrm(ks[14], (DEPTH, D_MODEL), 0.02),
        "w_ffn_in": nrm(ks[15], (DEPTH, D_MODEL, 2 * D_FF), D_MODEL ** -0.5),
        "w_ffn_out": nrm(ks[16], (DEPTH, D_FF, D_MODEL), D_FF ** -0.5),
        "final_norm": 1.0 + nrm(ks[17], (D_MODEL,), 0.02),
    }


def reference(x, mem, mix_norm, mem_norm, w_mem_kv, w_out, hg_w_in, hg_lb, hg_onorm,
              gm_w_in, gm_ln_g, gm_ln_b, gm_ws, gm_bs, ffn_norm, w_ffn_in, w_ffn_out,
              final_norm):
    lb_all = jnp.cumsum(jax.nn.softmax(hg_lb.astype(jnp.float32), axis=0), axis=0)
    for i in range(DEPTH):
        h = rmsnorm(x, mix_norm[i])
        j = i // N_MIXERS
        if i % N_MIXERS == 0:
            p = h @ hg_w_in[j]
            tok = hgrn2_mix(p[..., :4 * D_TOK], lb_all[i]) * hg_onorm[j].astype(jnp.float32)
            qm = p[..., 4 * D_TOK:]
        else:
            p = h @ gm_w_in[j]
            tok = gmlp_mix(p[..., :2 * D_TOK], gm_ln_g[j], gm_ln_b[j], gm_ws[j], gm_bs[j])
            qm = p[..., 2 * D_TOK:]
        mo = mem_attn(qm, mem, mem_norm[i], w_mem_kv[i])
        heads = jnp.concatenate([tok, mo], axis=-1).astype(x.dtype)
        x = x + heads @ w_out[i]
        x = x + swiglu(rmsnorm(x, ffn_norm[i]), w_ffn_in[i], w_ffn_out[i])
    return rmsnorm(x, final_norm)
```

```python
import functools

import jax
import jax.numpy as jnp
from jax import lax
from jax.experimental import pallas as pl
from jax.experimental.pallas import tpu as pltpu

F32 = jnp.float32
BF16 = jnp.bfloat16
EPS = 1e-6

LANES = 128
HG_EXPAND = 128
CHUNK = 64
LEVELS = (32, 16, 8)
DIAG = 8
GM_CHUNK = 128
MEM_HEADS = 4
SEQ_TILE = 256
FFN_TILE = 512
VMEM_LIMIT = 56 * 1024 * 1024

_NT = (((1,), (1,)), ((), ()))
_TN = (((0,), (0,)), ((), ()))


def _dot(a, b):
    return jnp.dot(a, b, preferred_element_type=F32)


def _rmsnorm(x, g):
    return x * lax.rsqrt(jnp.mean(x * x, axis=-1, keepdims=True) + EPS) * g


def _mem_kv_init(mem_ref, gmem_ref, wkv_ref, kbd_s, vbd_s):
    d_mem = vbd_s.shape[1]
    hdim = d_mem // MEM_HEADS
    mlen = kbd_s.shape[1] // MEM_HEADS
    mn = _rmsnorm(mem_ref[...], gmem_ref[...])
    kv = _dot(mn.astype(BF16), wkv_ref[...])
    k = kv[:, :d_mem] * (hdim ** -0.5)
    v = kv[:, d_mem:]
    kt = k.T
    row_head = lax.broadcasted_iota(jnp.int32, kt.shape, 0) // hdim
    col_head = lax.broadcasted_iota(jnp.int32, v.shape, 1) // hdim
    for h in range(MEM_HEADS):
        kbd_s[:, h * mlen:(h + 1) * mlen] = jnp.where(row_head == h, kt, 0.0).astype(BF16)
        vbd_s[h * mlen:(h + 1) * mlen, :] = jnp.where(col_head == h, v, 0.0).astype(BF16)


def _mem_attention(qm, kbd_s, vbd_s):
    mlen = kbd_s.shape[1] // MEM_HEADS
    s_all = _dot(qm.astype(BF16), kbd_s[...])
    probs = []
    for h in range(MEM_HEADS):
        sh = s_all[:, h * mlen:(h + 1) * mlen]
        e = jnp.exp(sh - jnp.max(sh, axis=-1, keepdims=True))
        probs.append((e / jnp.sum(e, axis=-1, keepdims=True)).astype(BF16))
    return _dot(jnp.concatenate(probs, axis=1), vbd_s[...])


def _hgrn_layer_kernel(x_ref, mem_ref, gmix_ref, gmem_ref, wkv_ref, win_ref, lb_ref, onorm_ref,
                       wout_ref, tri_ref, esel_ref, masks_ref, out_ref,
                       st_s, kbd_s, vbd_s, p_s, b_s, kk_s, v_s, qe_s, kd_s, xw_s, pd_s, rd_s,
                       ebl_s, o_s, heads_s, *, layer_idx):
    ts = x_ref.shape[0]
    d_tok = b_s.shape[1]
    n_heads = d_tok // HG_EXPAND
    n_chunks = ts // CHUNK

    @pl.when(pl.program_id(1) == 0)
    def _():
        st_s[...] = jnp.zeros_like(st_s)
        _mem_kv_init(mem_ref, gmem_ref, wkv_ref, kbd_s, vbd_s)

    x = x_ref[...]
    h = _rmsnorm(x, gmix_ref[...]).astype(BF16)
    p_s[...] = _dot(h, win_ref[...])

    lbl = lb_ref[...]
    lbe = jnp.exp(lbl - jnp.max(lbl, axis=0, keepdims=True))
    lbv = jnp.sum(lbe[:layer_idx + 1], axis=0, keepdims=True) / jnp.sum(lbe, axis=0, keepdims=True)

    f = lbv + (1.0 - lbv) * jax.nn.sigmoid(p_s[:, d_tok:2 * d_tok])
    lf = jnp.log(f)
    kk_s[...] = 1.0 - f
    hi = lf.astype(BF16)
    r1 = lf - hi.astype(F32)
    mid = r1.astype(BF16)
    lo = (r1 - mid.astype(F32)).astype(BF16)
    bsum = _dot(tri_ref[...], jnp.concatenate([hi, mid, lo], axis=1))
    b = bsum[:, :d_tok] + (bsum[:, d_tok:2 * d_tok] + bsum[:, 2 * d_tok:])
    b_s[...] = b
    qe_s[...] = (p_s[:, :d_tok] * jnp.exp(b)).astype(BF16)
    v_s[...] = p_s[:, 2 * d_tok:3 * d_tok].astype(BF16)

    for c in range(n_chunks):
        r0 = c * CHUNK
        rows = slice(r0, r0 + CHUNK)
        bl = b_s[r0 + CHUNK - 1:r0 + CHUNK, :]
        kd_s[rows, :] = (kk_s[rows, :] * jnp.exp(bl - b_s[rows, :])).astype(BF16)
        ebl_s[c:c + 1, :] = jnp.exp(bl)
        for li, w in enumerate(LEVELS):
            for g0 in range(r0, r0 + CHUNK, 2 * w):
                m = b_s[g0 + w:g0 + w + 1, :]
                xk = kk_s[g0:g0 + w, :] * jnp.exp(m - b_s[g0:g0 + w, :])
                xq = p_s[g0 + w:g0 + 2 * w, :d_tok] * jnp.exp(b_s[g0 + w:g0 + 2 * w, :] - m)
                xw_s[li, g0:g0 + 2 * w, :] = jnp.concatenate([xk, xq], axis=0).astype(BF16)

    def _pd_body(i, carry):
        r0 = pl.multiple_of(i * 2 * DIAG, 2 * DIAG)
        q16 = p_s[pl.ds(r0, 2 * DIAG), :d_tok]
        b16 = b_s[pl.ds(r0, 2 * DIAG), :]
        for c in range(DIAG):
            def partner(ref):
                top = jnp.broadcast_to(ref[pl.ds(r0 + c, 1), :], (DIAG, d_tok))
                bot = jnp.broadcast_to(ref[pl.ds(r0 + DIAG + c, 1), :], (DIAG, d_tok))
                return jnp.concatenate([top, bot], axis=0)
            val = q16 * partner(kk_s) * jnp.exp(jnp.minimum(b16 - partner(b_s), 0.0))
            vb = val.astype(BF16)
            for hd in range(n_heads):
                col = (hd * DIAG + c) * HG_EXPAND
                pd_s[pl.ds(r0, 2 * DIAG), col:col + HG_EXPAND] = vb[:, hd * HG_EXPAND:(hd + 1) * HG_EXPAND]
        return carry

    lax.fori_loop(0, ts // (2 * DIAG), _pd_body, 0)
    kdiag = DIAG * HG_EXPAND
    for hd in range(n_heads):
        rd_s[hd] = _dot(pd_s[:, hd * kdiag:(hd + 1) * kdiag], esel_ref[...])

    for c in range(n_chunks):
        rows = slice(c * CHUNK, (c + 1) * CHUNK)
        for hd in range(n_heads):
            cols = slice(hd * HG_EXPAND, (hd + 1) * HG_EXPAND)
            a = rd_s[hd, rows, :CHUNK] * masks_ref[len(LEVELS)]
            for li in range(len(LEVELS)):
                xl = xw_s[li, rows, cols]
                a = a + lax.dot_general(xl, xl, _NT, preferred_element_type=F32) * masks_ref[li]
            vc = v_s[rows, cols]
            st = st_s[hd]
            inter = lax.dot_general(qe_s[rows, cols], st.astype(BF16), _NT, preferred_element_type=F32)
            o_s[rows, cols] = inter + _dot(a.astype(BF16), vc)
            upd = lax.dot_general(vc, kd_s[rows, cols], _TN, preferred_element_type=F32)
            st_s[hd] = st * ebl_s[c:c + 1, cols] + upd

    gate = p_s[:, 3 * d_tok:4 * d_tok]
    gate = gate * jax.nn.sigmoid(gate)
    for hd in range(n_heads):
        cols = slice(hd * HG_EXPAND, (hd + 1) * HG_EXPAND)
        oh = o_s[:, cols]
        on = oh * lax.rsqrt(jnp.mean(oh * oh, axis=-1, keepdims=True) + EPS)
        heads_s[:, cols] = (on * gate[:, cols] * onorm_ref[:, cols]).astype(BF16)

    heads_s[:, d_tok:] = _mem_attention(p_s[:, 4 * d_tok:], kbd_s, vbd_s).astype(BF16)
    out_ref[...] = x + _dot(heads_s[...], wout_ref[...])


def _gmlp_layer_kernel(x_ref, mem_ref, gmix_ref, gmem_ref, wkv_ref, win_ref, lng_ref, lnb_ref,
                       ws_ref, bias_ref, wout_ref, out_ref, kbd_s, vbd_s, vn_s, heads_s):
    ts = x_ref.shape[0]
    d_tok = vn_s.shape[1]
    n_groups = ws_ref.shape[0]
    gdim = d_tok // n_groups

    @pl.when(pl.program_id(1) == 0)
    def _():
        _mem_kv_init(mem_ref, gmem_ref, wkv_ref, kbd_s, vbd_s)

    x = x_ref[...]
    h = _rmsnorm(x, gmix_ref[...]).astype(BF16)
    p = _dot(h, win_ref[...])
    pz = p[:, :2 * d_tok]
    z = 0.5 * pz * (1.0 + lax.erf(pz * (2.0 ** -0.5)))
    u = z[:, :d_tok]
    v = z[:, d_tok:]
    mu = jnp.mean(v, axis=-1, keepdims=True)
    vc = v - mu
    var = jnp.mean(vc * vc, axis=-1, keepdims=True)
    vn_s[...] = (vc * lax.rsqrt(var + EPS) * lng_ref[...] + lnb_ref[...]).astype(BF16)

    tril = (lax.broadcasted_iota(jnp.int32, (GM_CHUNK, GM_CHUNK), 0)
            >= lax.broadcasted_iota(jnp.int32, (GM_CHUNK, GM_CHUNK), 1))
    for g in range(n_groups):
        cols = slice(g * gdim, (g + 1) * gdim)
        wg = jnp.where(tril, ws_ref[g], 0.0).astype(BF16)
        for n in range(ts // GM_CHUNK):
            rows = slice(n * GM_CHUNK, (n + 1) * GM_CHUNK)
            sv = _dot(wg, vn_s[rows, cols]) + bias_ref[:, cols]
            heads_s[rows, cols] = (u[rows, cols] * sv).astype(BF16)

    heads_s[:, d_tok:] = _mem_attention(p[:, 2 * d_tok:], kbd_s, vbd_s).astype(BF16)
    out_ref[...] = x + _dot(heads_s[...], wout_ref[...])


def _ffn_kernel(x_ref, g_ref, w1_ref, w2_ref, gfin_ref, out_ref, *, final_norm):
    d_ff = w2_ref.shape[0]
    x = x_ref[...]
    h = _rmsnorm(x, g_ref[...]).astype(BF16)
    a = _dot(h, w1_ref[...])
    gate = a[:, :d_ff]
    act = (gate * jax.nn.sigmoid(gate) * a[:, d_ff:]).astype(BF16)
    y = x + _dot(act, w2_ref[...])
    if final_norm:
        y = _rmsnorm(y, gfin_ref[...])
    out_ref[...] = y


def _const_spec(shape):
    nd = len(shape)
    return pl.BlockSpec(shape, lambda *_: (0,) * nd)


def _mixer_specs(B, S, D, M, ts):
    x_spec = pl.BlockSpec((None, ts, D), lambda b, j: (b, j, 0))
    mem_spec = pl.BlockSpec((None, M, D), lambda b, j: (b, 0, 0))
    return x_spec, mem_spec


def _hgrn_layer(x, mem, gmix, gmem, wkv, win, lb, onorm, wout, layer_idx):
    B, S, D = x.shape
    M = mem.shape[1]
    d_tok = lb.shape[1]
    d_mem = D - d_tok
    n_heads = d_tok // HG_EXPAND
    ts = min(SEQ_TILE, S)
    assert S % ts == 0 and ts % CHUNK == 0 and d_tok % HG_EXPAND == 0

    r = jnp.arange(ts)
    tri = ((r[:, None] >= r[None, :]) & (r[:, None] // CHUNK == r[None, :] // CHUNK)).astype(BF16)
    kidx = jnp.arange(DIAG * HG_EXPAND) // HG_EXPAND
    esel = (kidx[:, None] == (jnp.arange(LANES)[None, :] % DIAG)).astype(BF16)
    t = jnp.arange(CHUNK)[:, None]
    s = jnp.arange(CHUNK)[None, :]
    masks = [(((t // w) % 2 == 1) & (s // w == t // w - 1)) for w in LEVELS]
    masks.append((s // DIAG == t // DIAG) & (s % DIAG <= t % DIAG))
    masks = jnp.stack(masks).astype(F32)

    x_spec, mem_spec = _mixer_specs(B, S, D, M, ts)
    kernel = functools.partial(_hgrn_layer_kernel, layer_idx=layer_idx)
    return pl.pallas_call(
        kernel,
        grid=(B, S // ts),
        in_specs=[x_spec, mem_spec, _const_spec((1, D)), _const_spec((1, D)), _const_spec(wkv.shape),
                  _const_spec(win.shape), _const_spec(lb.shape), _const_spec((1, d_tok)),
                  _const_spec(wout.shape), _const_spec(tri.shape), _const_spec(esel.shape),
                  _const_spec(masks.shape)],
        out_specs=x_spec,
        out_shape=jax.ShapeDtypeStruct(x.shape, F32),
        scratch_shapes=[
            pltpu.VMEM((n_heads, HG_EXPAND, HG_EXPAND), F32),
            pltpu.VMEM((d_mem, MEM_HEADS * M), BF16),
            pltpu.VMEM((MEM_HEADS * M, d_mem), BF16),
            pltpu.VMEM((ts, win.shape[1]), F32),
            pltpu.VMEM((ts, d_tok), F32),
            pltpu.VMEM((ts, d_tok), F32),
            pltpu.VMEM((ts, d_tok), BF16),
            pltpu.VMEM((ts, d_tok), BF16),
            pltpu.VMEM((ts, d_tok), BF16),
            pltpu.VMEM((len(LEVELS), ts, d_tok), BF16),
            pltpu.VMEM((ts, n_heads * DIAG * HG_EXPAND), BF16),
            pltpu.VMEM((n_heads, ts, LANES), F32),
            pltpu.VMEM((8, d_tok), F32),
            pltpu.VMEM((ts, d_tok), F32),
            pltpu.VMEM((ts, D), BF16),
        ],
        compiler_params=pltpu.CompilerParams(
            dimension_semantics=("arbitrary", "arbitrary"), vmem_limit_bytes=VMEM_LIMIT),
        name="hgrn_layer",
    )(x, mem, gmix.reshape(1, D), gmem.reshape(1, D), wkv, win, lb, onorm.reshape(1, d_tok), wout,
      tri, esel, masks)


def _gmlp_layer(x, mem, gmix, gmem, wkv, win, lng, lnb, ws, bs, wout):
    B, S, D = x.shape
    M = mem.shape[1]
    d_tok = lng.shape[0]
    d_mem = D - d_tok
    n_groups = ws.shape[0]
    gdim = d_tok // n_groups
    ts = min(SEQ_TILE, S)
    assert S % ts == 0 and ts % GM_CHUNK == 0 and ws.shape[1:] == (GM_CHUNK, GM_CHUNK)
    bias = jnp.repeat(bs.astype(F32).T, gdim, axis=1)

    x_spec, mem_spec = _mixer_specs(B, S, D, M, ts)
    return pl.pallas_call(
        _gmlp_layer_kernel,
        grid=(B, S // ts),
        in_specs=[x_spec, mem_spec, _const_spec((1, D)), _const_spec((1, D)), _const_spec(wkv.shape),
                  _const_spec(win.shape), _const_spec((1, d_tok)), _const_spec((1, d_tok)),
                  _const_spec(ws.shape), _const_spec(bias.shape), _const_spec(wout.shape)],
        out_specs=x_spec,
        out_shape=jax.ShapeDtypeStruct(x.shape, F32),
        scratch_shapes=[
            pltpu.VMEM((d_mem, MEM_HEADS * M), BF16),
            pltpu.VMEM((MEM_HEADS * M, d_mem), BF16),
            pltpu.VMEM((ts, d_tok), BF16),
            pltpu.VMEM((ts, D), BF16),
        ],
        compiler_params=pltpu.CompilerParams(
            dimension_semantics=("arbitrary", "arbitrary"), vmem_limit_bytes=VMEM_LIMIT),
        name="gmlp_layer",
    )(x, mem, gmix.reshape(1, D), gmem.reshape(1, D), wkv, win, lng.reshape(1, d_tok),
      lnb.reshape(1, d_tok), ws, bias, wout)


def _ffn_layer(x, g, w1, w2, gfin, final_norm):
    B, S, D = x.shape
    T = B * S
    tm = min(FFN_TILE, T)
    assert T % tm == 0
    x2 = x.reshape(T, D)
    row_spec = pl.BlockSpec((tm, D), lambda i: (i, 0))
    kernel = functools.partial(_ffn_kernel, final_norm=final_norm)
    out = pl.pallas_call(
        kernel,
        grid=(T // tm,),
        in_specs=[row_spec, _const_spec((1, D)), _const_spec(w1.shape), _const_spec(w2.shape),
                  _const_spec((1, D))],
        out_specs=row_spec,
        out_shape=jax.ShapeDtypeStruct((T, D), F32),
        compiler_params=pltpu.CompilerParams(
            dimension_semantics=("arbitrary",), vmem_limit_bytes=VMEM_LIMIT),
        name="ffn_layer",
    )(x2, g.reshape(1, D), w1, w2, gfin.reshape(1, D))
    return out.reshape(B, S, D)


def kernel(x, mem, mix_norm, mem_norm, w_mem_kv, w_out, hg_w_in, hg_lb, hg_onorm, gm_w_in, gm_ln_g,
           gm_ln_b, gm_ws, gm_bs, ffn_norm, w_ffn_in, w_ffn_out, final_norm):
    depth = mix_norm.shape[0]
    n_mixers = 2
    x = x.astype(F32)
    mem = mem.astype(F32)
    for i in range(depth):
        j = i // n_mixers
        wkv = w_mem_kv[i].astype(BF16)
        wout = w_out[i].astype(BF16)
        if i % n_mixers == 0:
            x = _hgrn_layer(x, mem, mix_norm[i], mem_norm[i], wkv, hg_w_in[j].astype(BF16),
                            hg_lb.astype(F32), hg_onorm[j], wout, layer_idx=i)
        else:
            x = _gmlp_layer(x, mem, mix_norm[i], mem_norm[i], wkv, gm_w_in[j].astype(BF16),
                            gm_ln_g[j], gm_ln_b[j], gm_ws[j], gm_bs[j], wout)
        x = _ffn_layer(x, ffn_norm[i], w_ffn_in[i].astype(BF16), w_ffn_out[i].astype(BF16),
                       final_norm, final_norm=(i == depth - 1))
    return x
```

```python
import functools

import jax
import jax.numpy as jnp
from jax import lax
from jax.experimental import pallas as pl
from jax.experimental.pallas import tpu as pltpu

F32 = jnp.float32
BF16 = jnp.bfloat16
EPS = 1e-6

LANES = 128
HG_EXPAND = 128
CHUNK = 64
LEVELS = (32, 16, 8)
DIAG = 8
DIRECT_MIN_B = -80.0
GM_CHUNK = 128
MEM_HEADS = 4
SEQ_TILE = 256
FFN_TILE = 512
VMEM_LIMIT = 56 * 1024 * 1024

_NT = (((1,), (1,)), ((), ()))
_TN = (((0,), (0,)), ((), ()))


def _dot(a, b):
    return jnp.dot(a, b, preferred_element_type=F32)


def _rmsnorm(x, g):
    return x * lax.rsqrt(jnp.mean(x * x, axis=-1, keepdims=True) + EPS) * g


def _mem_kv_init(mem_ref, gmem_ref, wkv_ref, kbd_s, vbd_s):
    d_mem = vbd_s.shape[1]
    hdim = d_mem // MEM_HEADS
    mlen = kbd_s.shape[1] // MEM_HEADS
    mn = _rmsnorm(mem_ref[...], gmem_ref[...])
    kv = _dot(mn.astype(BF16), wkv_ref[...])
    k = kv[:, :d_mem] * (hdim ** -0.5)
    v = kv[:, d_mem:]
    kt = k.T
    row_head = lax.broadcasted_iota(jnp.int32, kt.shape, 0) // hdim
    col_head = lax.broadcasted_iota(jnp.int32, v.shape, 1) // hdim
    for h in range(MEM_HEADS):
        kbd_s[:, h * mlen:(h + 1) * mlen] = jnp.where(row_head == h, kt, 0.0).astype(BF16)
        vbd_s[h * mlen:(h + 1) * mlen, :] = jnp.where(col_head == h, v, 0.0).astype(BF16)


def _mem_attention(qm, kbd_s, vbd_s):
    mlen = kbd_s.shape[1] // MEM_HEADS
    s_all = _dot(qm.astype(BF16), kbd_s[...])
    probs = []
    for h in range(MEM_HEADS):
        sh = s_all[:, h * mlen:(h + 1) * mlen]
        e = jnp.exp(sh - jnp.max(sh, axis=-1, keepdims=True))
        probs.append((e / jnp.sum(e, axis=-1, keepdims=True)).astype(BF16))
    return _dot(jnp.concatenate(probs, axis=1), vbd_s[...])


def _hgrn_layer_kernel(x_ref, mem_ref, gmix_ref, gmem_ref, wkv_ref, win_ref, lb_ref, onorm_ref,
                       wout_ref, tri_ref, esel_ref, masks_ref, out_ref,
                       st_s, kbd_s, vbd_s, p_s, b_s, kk_s, v_s, qe_s, kd_s, xw_s, pd_s, rd_s,
                       ebl_s, o_s, heads_s, *, layer_idx):
    ts = x_ref.shape[0]
    d_tok = b_s.shape[1]
    n_heads = d_tok // HG_EXPAND
    n_chunks = ts // CHUNK

    @pl.when(pl.program_id(1) == 0)
    def _():
        st_s[...] = jnp.zeros_like(st_s)
        _mem_kv_init(mem_ref, gmem_ref, wkv_ref, kbd_s, vbd_s)

    x = x_ref[...]
    h = _rmsnorm(x, gmix_ref[...]).astype(BF16)
    p_s[...] = _dot(h, win_ref[...])

    lbl = lb_ref[...]
    lbe = jnp.exp(lbl - jnp.max(lbl, axis=0, keepdims=True))
    lbv = jnp.sum(lbe[:layer_idx + 1], axis=0, keepdims=True) / jnp.sum(lbe, axis=0, keepdims=True)

    f = lbv + (1.0 - lbv) * jax.nn.sigmoid(p_s[:, d_tok:2 * d_tok])
    lf = jnp.log(f)
    kk_s[...] = 1.0 - f
    hi = lf.astype(BF16)
    r1 = lf - hi.astype(F32)
    mid = r1.astype(BF16)
    lo = (r1 - mid.astype(F32)).astype(BF16)
    bsum = _dot(tri_ref[...], jnp.concatenate([hi, mid, lo], axis=1))
    b = bsum[:, :d_tok] + (bsum[:, d_tok:2 * d_tok] + bsum[:, 2 * d_tok:])
    b_s[...] = b
    qe_s[...] = (p_s[:, :d_tok] * jnp.exp(b)).astype(BF16)
    v_s[...] = p_s[:, 2 * d_tok:3 * d_tok].astype(BF16)

    for c in range(n_chunks):
        rows = slice(c * CHUNK, (c + 1) * CHUNK)
        bl = b_s[(c + 1) * CHUNK - 1:(c + 1) * CHUNK, :]
        kd_s[rows, :] = (kk_s[rows, :] * jnp.exp(bl - b_s[rows, :])).astype(BF16)
        ebl_s[c:c + 1, :] = jnp.exp(bl)

    def recurrence(scores_fn):
        for c in range(n_chunks):
            rows = slice(c * CHUNK, (c + 1) * CHUNK)
            for hd in range(n_heads):
                cols = slice(hd * HG_EXPAND, (hd + 1) * HG_EXPAND)
                a = scores_fn(hd, rows, cols)
                vc = v_s[rows, cols]
                st = st_s[hd]
                inter = lax.dot_general(qe_s[rows, cols], st.astype(BF16), _NT, preferred_element_type=F32)
                o_s[rows, cols] = inter + _dot(a.astype(BF16), vc)
                upd = lax.dot_general(vc, kd_s[rows, cols], _TN, preferred_element_type=F32)
                st_s[hd] = st * ebl_s[c:c + 1, cols] + upd

    direct_ok = jnp.min(b) >= DIRECT_MIN_B

    @pl.when(direct_ok)
    def _():
        kx_s = xw_s.at[0]
        kx_s[...] = (kk_s[...] * jnp.exp(-b_s[...])).astype(BF16)
        causal = masks_ref[len(LEVELS) + 1] != 0.0

        def scores(hd, rows, cols):
            s = lax.dot_general(qe_s[rows, cols], kx_s[rows, cols], _NT, preferred_element_type=F32)
            return jnp.where(causal, s, 0.0)

        recurrence(scores)

    @pl.when(jnp.logical_not(direct_ok))
    def _():
        for li, w in enumerate(LEVELS):
            for g0 in range(0, ts, 2 * w):
                m = b_s[g0 + w:g0 + w + 1, :]
                xk = kk_s[g0:g0 + w, :] * jnp.exp(m - b_s[g0:g0 + w, :])
                xq = p_s[g0 + w:g0 + 2 * w, :d_tok] * jnp.exp(b_s[g0 + w:g0 + 2 * w, :] - m)
                xw_s[li, g0:g0 + 2 * w, :] = jnp.concatenate([xk, xq], axis=0).astype(BF16)

        def pd_body(i, carry):
            r0 = pl.multiple_of(i * 2 * DIAG, 2 * DIAG)
            q16 = p_s[pl.ds(r0, 2 * DIAG), :d_tok]
            b16 = b_s[pl.ds(r0, 2 * DIAG), :]
            for c in range(DIAG):
                def partner(ref):
                    top = jnp.broadcast_to(ref[pl.ds(r0 + c, 1), :], (DIAG, d_tok))
                    bot = jnp.broadcast_to(ref[pl.ds(r0 + DIAG + c, 1), :], (DIAG, d_tok))
                    return jnp.concatenate([top, bot], axis=0)
                val = q16 * partner(kk_s) * jnp.exp(jnp.minimum(b16 - partner(b_s), 0.0))
                vb = val.astype(BF16)
                for hd in range(n_heads):
                    col = (hd * DIAG + c) * HG_EXPAND
                    pd_s[pl.ds(r0, 2 * DIAG), col:col + HG_EXPAND] = vb[:, hd * HG_EXPAND:(hd + 1) * HG_EXPAND]
            return carry

        lax.fori_loop(0, ts // (2 * DIAG), pd_body, 0)
        kdiag = DIAG * HG_EXPAND
        for hd in range(n_heads):
            rd_s[hd] = _dot(pd_s[:, hd * kdiag:(hd + 1) * kdiag], esel_ref[...])

        def scores(hd, rows, cols):
            a = rd_s[hd, rows, :CHUNK] * masks_ref[len(LEVELS)]
            for li in range(len(LEVELS)):
                xl = xw_s[li, rows, cols]
                a = a + lax.dot_general(xl, xl, _NT, preferred_element_type=F32) * masks_ref[li]
            return a

        recurrence(scores)

    gate = p_s[:, 3 * d_tok:4 * d_tok]
    gate = gate * jax.nn.sigmoid(gate)
    for hd in range(n_heads):
        cols = slice(hd * HG_EXPAND, (hd + 1) * HG_EXPAND)
        oh = o_s[:, cols]
        on = oh * lax.rsqrt(jnp.mean(oh * oh, axis=-1, keepdims=True) + EPS)
        heads_s[:, cols] = (on * gate[:, cols] * onorm_ref[:, cols]).astype(BF16)

    heads_s[:, d_tok:] = _mem_attention(p_s[:, 4 * d_tok:], kbd_s, vbd_s).astype(BF16)
    out_ref[...] = x + _dot(heads_s[...], wout_ref[...])


def _gmlp_layer_kernel(x_ref, mem_ref, gmix_ref, gmem_ref, wkv_ref, win_ref, lng_ref, lnb_ref,
                       ws_ref, bias_ref, wout_ref, out_ref, kbd_s, vbd_s, vn_s, heads_s):
    ts = x_ref.shape[0]
    d_tok = vn_s.shape[1]
    n_groups = ws_ref.shape[0]
    gdim = d_tok // n_groups

    @pl.when(pl.program_id(1) == 0)
    def _():
        _mem_kv_init(mem_ref, gmem_ref, wkv_ref, kbd_s, vbd_s)

    x = x_ref[...]
    h = _rmsnorm(x, gmix_ref[...]).astype(BF16)
    p = _dot(h, win_ref[...])
    pz = p[:, :2 * d_tok]
    z = 0.5 * pz * (1.0 + lax.erf(pz * (2.0 ** -0.5)))
    u = z[:, :d_tok]
    v = z[:, d_tok:]
    mu = jnp.mean(v, axis=-1, keepdims=True)
    vc = v - mu
    var = jnp.mean(vc * vc, axis=-1, keepdims=True)
    vn_s[...] = (vc * lax.rsqrt(var + EPS) * lng_ref[...] + lnb_ref[...]).astype(BF16)

    tril = (lax.broadcasted_iota(jnp.int32, (GM_CHUNK, GM_CHUNK), 0)
            >= lax.broadcasted_iota(jnp.int32, (GM_CHUNK, GM_CHUNK), 1))
    for g in range(n_groups):
        cols = slice(g * gdim, (g + 1) * gdim)
        wg = jnp.where(tril, ws_ref[g], 0.0).astype(BF16)
        for n in range(ts // GM_CHUNK):
            rows = slice(n * GM_CHUNK, (n + 1) * GM_CHUNK)
            sv = _dot(wg, vn_s[rows, cols]) + bias_ref[:, cols]
            heads_s[rows, cols] = (u[rows, cols] * sv).astype(BF16)

    heads_s[:, d_tok:] = _mem_attention(p[:, 2 * d_tok:], kbd_s, vbd_s).astype(BF16)
    out_ref[...] = x + _dot(heads_s[...], wout_ref[...])


def _ffn_kernel(x_ref, g_ref, w1_ref, w2_ref, gfin_ref, out_ref, *, final_norm):
    d_ff = w2_ref.shape[0]
    x = x_ref[...]
    h = _rmsnorm(x, g_ref[...]).astype(BF16)
    a = _dot(h, w1_ref[...])
    gate = a[:, :d_ff]
    act = (gate * jax.nn.sigmoid(gate) * a[:, d_ff:]).astype(BF16)
    y = x + _dot(act, w2_ref[...])
    if final_norm:
        y = _rmsnorm(y, gfin_ref[...])
    out_ref[...] = y


def _const_spec(shape):
    nd = len(shape)
    return pl.BlockSpec(shape, lambda *_: (0,) * nd)


def _mixer_specs(B, S, D, M, ts):
    x_spec = pl.BlockSpec((None, ts, D), lambda b, j: (b, j, 0))
    mem_spec = pl.BlockSpec((None, M, D), lambda b, j: (b, 0, 0))
    return x_spec, mem_spec


def _hgrn_layer(x, mem, gmix, gmem, wkv, win, lb, onorm, wout, layer_idx):
    B, S, D = x.shape
    M = mem.shape[1]
    d_tok = lb.shape[1]
    d_mem = D - d_tok
    n_heads = d_tok // HG_EXPAND
    ts = min(SEQ_TILE, S)
    assert S % ts == 0 and ts % CHUNK == 0 and d_tok % HG_EXPAND == 0

    r = jnp.arange(ts)
    tri = ((r[:, None] >= r[None, :]) & (r[:, None] // CHUNK == r[None, :] // CHUNK)).astype(BF16)
    kidx = jnp.arange(DIAG * HG_EXPAND) // HG_EXPAND
    esel = (kidx[:, None] == (jnp.arange(LANES)[None, :] % DIAG)).astype(BF16)
    t = jnp.arange(CHUNK)[:, None]
    s = jnp.arange(CHUNK)[None, :]
    masks = [(((t // w) % 2 == 1) & (s // w == t // w - 1)) for w in LEVELS]
    masks.append((s // DIAG == t // DIAG) & (s % DIAG <= t % DIAG))
    masks.append(s <= t)
    masks = jnp.stack(masks).astype(F32)

    x_spec, mem_spec = _mixer_specs(B, S, D, M, ts)
    kernel = functools.partial(_hgrn_layer_kernel, layer_idx=layer_idx)
    return pl.pallas_call(
        kernel,
        grid=(B, S // ts),
        in_specs=[x_spec, mem_spec, _const_spec((1, D)), _const_spec((1, D)), _const_spec(wkv.shape),
                  _const_spec(win.shape), _const_spec(lb.shape), _const_spec((1, d_tok)),
                  _const_spec(wout.shape), _const_spec(tri.shape), _const_spec(esel.shape),
                  _const_spec(masks.shape)],
        out_specs=x_spec,
        out_shape=jax.ShapeDtypeStruct(x.shape, F32),
        scratch_shapes=[
            pltpu.VMEM((n_heads, HG_EXPAND, HG_EXPAND), F32),
            pltpu.VMEM((d_mem, MEM_HEADS * M), BF16),
            pltpu.VMEM((MEM_HEADS * M, d_mem), BF16),
            pltpu.VMEM((ts, win.shape[1]), F32),
            pltpu.VMEM((ts, d_tok), F32),
            pltpu.VMEM((ts, d_tok), F32),
            pltpu.VMEM((ts, d_tok), BF16),
            pltpu.VMEM((ts, d_tok), BF16),
            pltpu.VMEM((ts, d_tok), BF16),
            pltpu.VMEM((len(LEVELS), ts, d_tok), BF16),
            pltpu.VMEM((ts, n_heads * DIAG * HG_EXPAND), BF16),
            pltpu.VMEM((n_heads, ts, LANES), F32),
            pltpu.VMEM((8, d_tok), F32),
            pltpu.VMEM((ts, d_tok), F32),
            pltpu.VMEM((ts, D), BF16),
        ],
        compiler_params=pltpu.CompilerParams(
            dimension_semantics=("arbitrary", "arbitrary"), vmem_limit_bytes=VMEM_LIMIT),
        name="hgrn_layer",
    )(x, mem, gmix.reshape(1, D), gmem.reshape(1, D), wkv, win, lb, onorm.reshape(1, d_tok), wout,
      tri, esel, masks)


def _gmlp_layer(x, mem, gmix, gmem, wkv, win, lng, lnb, ws, bs, wout):
    B, S, D = x.shape
    M = mem.shape[1]
    d_tok = lng.shape[0]
    d_mem = D - d_tok
    n_groups = ws.shape[0]
    gdim = d_tok // n_groups
    ts = min(SEQ_TILE, S)
    assert S % ts == 0 and ts % GM_CHUNK == 0 and ws.shape[1:] == (GM_CHUNK, GM_CHUNK)
    bias = jnp.repeat(bs.astype(F32).T, gdim, axis=1)

    x_spec, mem_spec = _mixer_specs(B, S, D, M, ts)
    return pl.pallas_call(
        _gmlp_layer_kernel,
        grid=(B, S // ts),
        in_specs=[x_spec, mem_spec, _const_spec((1, D)), _const_spec((1, D)), _const_spec(wkv.shape),
                  _const_spec(win.shape), _const_spec((1, d_tok)), _const_spec((1, d_tok)),
                  _const_spec(ws.shape), _const_spec(bias.shape), _const_spec(wout.shape)],
        out_specs=x_spec,
        out_shape=jax.ShapeDtypeStruct(x.shape, F32),
        scratch_shapes=[
            pltpu.VMEM((d_mem, MEM_HEADS * M), BF16),
            pltpu.VMEM((MEM_HEADS * M, d_mem), BF16),
            pltpu.VMEM((ts, d_tok), BF16),
            pltpu.VMEM((ts, D), BF16),
        ],
        compiler_params=pltpu.CompilerParams(
            dimension_semantics=("arbitrary", "arbitrary"), vmem_limit_bytes=VMEM_LIMIT),
        name="gmlp_layer",
    )(x, mem, gmix.reshape(1, D), gmem.reshape(1, D), wkv, win, lng.reshape(1, d_tok),
      lnb.reshape(1, d_tok), ws, bias, wout)


def _ffn_layer(x, g, w1, w2, gfin, final_norm):
    B, S, D = x.shape
    T = B * S
    tm = min(FFN_TILE, T)
    assert T % tm == 0
    x2 = x.reshape(T, D)
    row_spec = pl.BlockSpec((tm, D), lambda i: (i, 0))
    kernel = functools.partial(_ffn_kernel, final_norm=final_norm)
    out = pl.pallas_call(
        kernel,
        grid=(T // tm,),
        in_specs=[row_spec, _const_spec((1, D)), _const_spec(w1.shape), _const_spec(w2.shape),
                  _const_spec((1, D))],
        out_specs=row_spec,
        out_shape=jax.ShapeDtypeStruct((T, D), F32),
        compiler_params=pltpu.CompilerParams(
            dimension_semantics=("arbitrary",), vmem_limit_bytes=VMEM_LIMIT),
        name="ffn_layer",
    )(x2, g.reshape(1, D), w1, w2, gfin.reshape(1, D))
    return out.reshape(B, S, D)


def kernel(x, mem, mix_norm, mem_norm, w_mem_kv, w_out, hg_w_in, hg_lb, hg_onorm, gm_w_in, gm_ln_g,
           gm_ln_b, gm_ws, gm_bs, ffn_norm, w_ffn_in, w_ffn_out, final_norm):
    depth = mix_norm.shape[0]
    n_mixers = 2
    x = x.astype(F32)
    mem = mem.astype(F32)
    for i in range(depth):
        j = i // n_mixers
        wkv = w_mem_kv[i].astype(BF16)
        wout = w_out[i].astype(BF16)
        if i % n_mixers == 0:
            x = _hgrn_layer(x, mem, mix_norm[i], mem_norm[i], wkv, hg_w_in[j].astype(BF16),
                            hg_lb.astype(F32), hg_onorm[j], wout, layer_idx=i)
        else:
            x = _gmlp_layer(x, mem, mix_norm[i], mem_norm[i], wkv, gm_w_in[j].astype(BF16),
                            gm_ln_g[j], gm_ln_b[j], gm_ws[j], gm_bs[j], wout)
        x = _ffn_layer(x, ffn_norm[i], w_ffn_in[i].astype(BF16), w_ffn_out[i].astype(BF16),
                       final_norm, final_norm=(i == depth - 1))
    return x
```

```python
import functools

import jax
import jax.numpy as jnp
from jax import lax
from jax.experimental import pallas as pl
from jax.experimental.pallas import tpu as pltpu

F32 = jnp.float32
BF16 = jnp.bfloat16
EPS = 1e-6

LANES = 128
SUBLANES = 8
HG_EXPAND = 128
CHUNK = 64
LEVELS = (32, 16, 8)
DIAG = 8
DIRECT_MIN_B = -80.0
GM_CHUNK = 128
MEM_HEADS = 4
SEQ_TILE = 256
FFN_TILE = 512
VMEM_LIMIT = 56 * 1024 * 1024

_NT = (((1,), (1,)), ((), ()))
_TN = (((0,), (0,)), ((), ()))


def _dot(a, b):
    return jnp.dot(a, b, preferred_element_type=F32)


def _rmsnorm(x, g):
    return x * lax.rsqrt(jnp.mean(x * x, axis=-1, keepdims=True) + EPS) * g


def _mem_kv_init(mem_ref, gmem_ref, wkv_ref, kbd_s, vbd_s):
    d_mem = vbd_s.shape[1]
    hdim = d_mem // MEM_HEADS
    mlen = kbd_s.shape[1] // MEM_HEADS
    mn = _rmsnorm(mem_ref[...], gmem_ref[...])
    kv = _dot(mn.astype(BF16), wkv_ref[...])
    k = kv[:, :d_mem] * (hdim ** -0.5)
    v = kv[:, d_mem:]
    kt = k.T
    row_head = lax.broadcasted_iota(jnp.int32, kt.shape, 0) // hdim
    col_head = lax.broadcasted_iota(jnp.int32, v.shape, 1) // hdim
    for h in range(MEM_HEADS):
        kbd_s[:, h * mlen:(h + 1) * mlen] = jnp.where(row_head == h, kt, 0.0).astype(BF16)
        vbd_s[h * mlen:(h + 1) * mlen, :] = jnp.where(col_head == h, v, 0.0).astype(BF16)


def _mem_attention(qm, kbd_s, vbd_s):
    mlen = kbd_s.shape[1] // MEM_HEADS
    s_all = _dot(qm.astype(BF16), kbd_s[...])
    probs = []
    for h in range(MEM_HEADS):
        sh = s_all[:, h * mlen:(h + 1) * mlen]
        e = jnp.exp(sh - jnp.max(sh, axis=-1, keepdims=True))
        probs.append((e / jnp.sum(e, axis=-1, keepdims=True)).astype(BF16))
    return _dot(jnp.concatenate(probs, axis=1), vbd_s[...])


def _hgrn_layer_kernel(x_ref, mem_ref, gmix_ref, gmem_ref, wkv_ref, win_ref, lb_ref, onorm_ref,
                       wout_ref, tri_ref, esel_ref, masks_ref, out_ref,
                       skv_s, kbd_s, vbd_s, p_s, b_s, kk_s, v_s, qe_s, kd_s, xw_s, pd_s, rd_s,
                       ebl_s, o_s, gm_s, mo_s, xprev_s, *, layer_idx):
    ts = x_ref.shape[0]
    d_tok = b_s.shape[1]
    n_heads = d_tok // HG_EXPAND
    n_chunks = ts // CHUNK
    j = pl.program_id(1)
    n_tiles = pl.num_programs(1) - 1

    @pl.when(j == 0)
    def _():
        skv_s[...] = jnp.zeros_like(skv_s)
        _mem_kv_init(mem_ref, gmem_ref, wkv_ref, kbd_s, vbd_s)
        o_s[...] = jnp.zeros_like(o_s)
        gm_s[...] = jnp.zeros_like(gm_s)
        mo_s[...] = jnp.zeros_like(mo_s)
        xprev_s[...] = jnp.zeros_like(xprev_s)
        ebl_s[...] = jnp.zeros_like(ebl_s)

    x = x_ref[...]
    h = _rmsnorm(x, gmix_ref[...]).astype(BF16)
    p_s[...] = _dot(h, win_ref[...])

    heads = []
    for hd in range(n_heads):
        cols = slice(hd * HG_EXPAND, (hd + 1) * HG_EXPAND)
        oh = o_s[:, cols]
        on = oh * lax.rsqrt(jnp.mean(oh * oh, axis=-1, keepdims=True) + EPS)
        heads.append((on * gm_s[:, cols]).astype(BF16))
    heads.append(mo_s[...])
    out_ref[...] = xprev_s[...] + _dot(jnp.concatenate(heads, axis=1), wout_ref[...])
    xprev_s[...] = x

    @pl.when(j < n_tiles)
    def _():
        lbl = lb_ref[...]
        lbe = jnp.exp(lbl - jnp.max(lbl, axis=0, keepdims=True))
        lbv = jnp.sum(lbe[:layer_idx + 1], axis=0, keepdims=True) / jnp.sum(lbe, axis=0, keepdims=True)

        f = lbv + (1.0 - lbv) * jax.nn.sigmoid(p_s[:, d_tok:2 * d_tok])
        lf = jnp.log(f)
        kk_s[...] = 1.0 - f
        hi = lf.astype(BF16)
        r1 = lf - hi.astype(F32)
        mid = r1.astype(BF16)
        lo = (r1 - mid.astype(F32)).astype(BF16)
        bsum = _dot(tri_ref[...], jnp.concatenate([hi, mid, lo], axis=1))
        b = bsum[:, :d_tok] + (bsum[:, d_tok:2 * d_tok] + bsum[:, 2 * d_tok:])
        b_s[...] = b
        qe_s[...] = (p_s[:, :d_tok] * jnp.exp(b)).astype(BF16)
        v_s[...] = p_s[:, 2 * d_tok:3 * d_tok].astype(BF16)

        for c in range(n_chunks):
            rows = slice(c * CHUNK, (c + 1) * CHUNK)
            bl = b_s[(c + 1) * CHUNK - 1:(c + 1) * CHUNK, :]
            kd_s[rows, :] = (kk_s[rows, :] * jnp.exp(bl - b_s[rows, :])).astype(BF16)
            ebl_s[c:c + 1, :] = jnp.exp(bl)

        def mix(scores_fn):
            et = ebl_s[...].T
            for c in range(n_chunks):
                rows = slice(c * CHUNK, (c + 1) * CHUNK)
                for hd in range(n_heads):
                    cols = slice(hd * HG_EXPAND, (hd + 1) * HG_EXPAND)
                    a = scores_fn(hd, rows, cols).astype(BF16)
                    vc = v_s[rows, cols]
                    skv = skv_s[hd]
                    lhs = jnp.concatenate([qe_s[rows, cols], a], axis=1)
                    rhs = jnp.concatenate([skv.astype(BF16), vc], axis=0)
                    o_s[rows, cols] = _dot(lhs, rhs)
                    upd = lax.dot_general(kd_s[rows, cols], vc, _TN, preferred_element_type=F32)
                    ecol = jnp.broadcast_to(et[hd * HG_EXPAND:(hd + 1) * HG_EXPAND, c:c + 1],
                                            (HG_EXPAND, HG_EXPAND))
                    skv_s[hd] = skv * ecol + upd
            gate = p_s[:, 3 * d_tok:4 * d_tok]
            gm_s[...] = gate * jax.nn.sigmoid(gate) * onorm_ref[...]
            mo_s[...] = _mem_attention(p_s[:, 4 * d_tok:], kbd_s, vbd_s).astype(BF16)

        direct_ok = jnp.min(b) >= DIRECT_MIN_B

        @pl.when(direct_ok)
        def _():
            kx_s = xw_s.at[0]
            kx_s[...] = (kk_s[...] * jnp.exp(-b_s[...])).astype(BF16)
            causal = masks_ref[len(LEVELS) + 1] != 0.0

            def scores(hd, rows, cols):
                s = lax.dot_general(qe_s[rows, cols], kx_s[rows, cols], _NT, preferred_element_type=F32)
                return jnp.where(causal, s, 0.0)

            mix(scores)

        @pl.when(jnp.logical_not(direct_ok))
        def _():
            for li, w in enumerate(LEVELS):
                for g0 in range(0, ts, 2 * w):
                    m = b_s[g0 + w:g0 + w + 1, :]
                    xk = kk_s[g0:g0 + w, :] * jnp.exp(m - b_s[g0:g0 + w, :])
                    xq = p_s[g0 + w:g0 + 2 * w, :d_tok] * jnp.exp(b_s[g0 + w:g0 + 2 * w, :] - m)
                    xw_s[li, g0:g0 + 2 * w, :] = jnp.concatenate([xk, xq], axis=0).astype(BF16)

            def pd_body(i, carry):
                r0 = pl.multiple_of(i * 2 * DIAG, 2 * DIAG)
                q16 = p_s[pl.ds(r0, 2 * DIAG), :d_tok]
                b16 = b_s[pl.ds(r0, 2 * DIAG), :]
                for c in range(DIAG):
                    def partner(ref):
                        top = jnp.broadcast_to(ref[pl.ds(r0 + c, 1), :], (DIAG, d_tok))
                        bot = jnp.broadcast_to(ref[pl.ds(r0 + DIAG + c, 1), :], (DIAG, d_tok))
                        return jnp.concatenate([top, bot], axis=0)
                    val = q16 * partner(kk_s) * jnp.exp(jnp.minimum(b16 - partner(b_s), 0.0))
                    vb = val.astype(BF16)
                    for hd in range(n_heads):
                        col = (hd * DIAG + c) * HG_EXPAND
                        pd_s[pl.ds(r0, 2 * DIAG), col:col + HG_EXPAND] = vb[:, hd * HG_EXPAND:(hd + 1) * HG_EXPAND]
                return carry

            lax.fori_loop(0, ts // (2 * DIAG), pd_body, 0)
            kdiag = DIAG * HG_EXPAND
            for hd in range(n_heads):
                rd_s[hd] = _dot(pd_s[:, hd * kdiag:(hd + 1) * kdiag], esel_ref[...])

            def scores(hd, rows, cols):
                a = rd_s[hd, rows, :CHUNK] * masks_ref[len(LEVELS)]
                for li in range(len(LEVELS)):
                    xl = xw_s[li, rows, cols]
                    a = a + lax.dot_general(xl, xl, _NT, preferred_element_type=F32) * masks_ref[li]
                return a

            mix(scores)


def _gmlp_layer_kernel(x_ref, mem_ref, gmix_ref, gmem_ref, wkv_ref, win_ref, lng_ref, lnb_ref,
                       ws_ref, bias_ref, wout_ref, out_ref, kbd_s, vbd_s, vn_s, heads_s):
    ts = x_ref.shape[0]
    d_tok = vn_s.shape[1]
    n_groups = ws_ref.shape[0]
    gdim = d_tok // n_groups

    @pl.when(pl.program_id(1) == 0)
    def _():
        _mem_kv_init(mem_ref, gmem_ref, wkv_ref, kbd_s, vbd_s)

    x = x_ref[...]
    h = _rmsnorm(x, gmix_ref[...]).astype(BF16)
    p = _dot(h, win_ref[...])
    pz = p[:, :2 * d_tok]
    z = 0.5 * pz * (1.0 + lax.erf(pz * (2.0 ** -0.5)))
    u = z[:, :d_tok]
    v = z[:, d_tok:]
    mu = jnp.mean(v, axis=-1, keepdims=True)
    vc = v - mu
    var = jnp.mean(vc * vc, axis=-1, keepdims=True)
    vn_s[...] = (vc * lax.rsqrt(var + EPS) * lng_ref[...] + lnb_ref[...]).astype(BF16)

    tril = (lax.broadcasted_iota(jnp.int32, (GM_CHUNK, GM_CHUNK), 0)
            >= lax.broadcasted_iota(jnp.int32, (GM_CHUNK, GM_CHUNK), 1))
    for g in range(n_groups):
        cols = slice(g * gdim, (g + 1) * gdim)
        wg = jnp.where(tril, ws_ref[g], 0.0).astype(BF16)
        for n in range(ts // GM_CHUNK):
            rows = slice(n * GM_CHUNK, (n + 1) * GM_CHUNK)
            sv = _dot(wg, vn_s[rows, cols]) + bias_ref[:, cols]
            heads_s[rows, cols] = (u[rows, cols] * sv).astype(BF16)

    heads_s[:, d_tok:] = _mem_attention(p[:, 2 * d_tok:], kbd_s, vbd_s).astype(BF16)
    out_ref[...] = x + _dot(heads_s[...], wout_ref[...])


def _ffn_kernel(x_ref, g_ref, w1_ref, w2_ref, gfin_ref, out_ref, *, final_norm):
    d_ff = w2_ref.shape[0]
    x = x_ref[...]
    h = _rmsnorm(x, g_ref[...]).astype(BF16)
    a = _dot(h, w1_ref[...])
    gate = a[:, :d_ff]
    act = (gate * jax.nn.sigmoid(gate) * a[:, d_ff:]).astype(BF16)
    y = x + _dot(act, w2_ref[...])
    if final_norm:
        y = _rmsnorm(y, gfin_ref[...])
    out_ref[...] = y


def _const_spec(shape):
    nd = len(shape)
    return pl.BlockSpec(shape, lambda *_: (0,) * nd)


def _hgrn_layer(x, mem, gmix, gmem, wkv, win, lb, onorm, wout, layer_idx):
    B, S, D = x.shape
    M = mem.shape[1]
    d_tok = lb.shape[1]
    d_mem = D - d_tok
    n_heads = d_tok // HG_EXPAND
    ts = min(SEQ_TILE, S)
    n_tiles = S // ts
    assert S % ts == 0 and ts % CHUNK == 0 and d_tok % HG_EXPAND == 0 and ts // CHUNK <= SUBLANES

    r = jnp.arange(ts)
    tri = ((r[:, None] >= r[None, :]) & (r[:, None] // CHUNK == r[None, :] // CHUNK)).astype(BF16)
    kidx = jnp.arange(DIAG * HG_EXPAND) // HG_EXPAND
    esel = (kidx[:, None] == (jnp.arange(LANES)[None, :] % DIAG)).astype(BF16)
    t = jnp.arange(CHUNK)[:, None]
    s = jnp.arange(CHUNK)[None, :]
    masks = [(((t // w) % 2 == 1) & (s // w == t // w - 1)) for w in LEVELS]
    masks.append((s // DIAG == t // DIAG) & (s % DIAG <= t % DIAG))
    masks.append(s <= t)
    masks = jnp.stack(masks).astype(F32)

    x_spec = pl.BlockSpec((None, ts, D), lambda b, j: (b, jnp.minimum(j, n_tiles - 1), 0))
    out_spec = pl.BlockSpec((None, ts, D), lambda b, j: (b, jnp.maximum(j - 1, 0), 0))
    mem_spec = pl.BlockSpec((None, M, D), lambda b, j: (b, 0, 0))
    kernel = functools.partial(_hgrn_layer_kernel, layer_idx=layer_idx)
    return pl.pallas_call(
        kernel,
        grid=(B, n_tiles + 1),
        in_specs=[x_spec, mem_spec, _const_spec((1, D)), _const_spec((1, D)), _const_spec(wkv.shape),
                  _const_spec(win.shape), _const_spec(lb.shape), _const_spec((1, d_tok)),
                  _const_spec(wout.shape), _const_spec(tri.shape), _const_spec(esel.shape),
                  _const_spec(masks.shape)],
        out_specs=out_spec,
        out_shape=jax.ShapeDtypeStruct(x.shape, F32),
        scratch_shapes=[
            pltpu.VMEM((n_heads, HG_EXPAND, HG_EXPAND), F32),
            pltpu.VMEM((d_mem, MEM_HEADS * M), BF16),
            pltpu.VMEM((MEM_HEADS * M, d_mem), BF16),
            pltpu.VMEM((ts, win.shape[1]), F32),
            pltpu.VMEM((ts, d_tok), F32),
            pltpu.VMEM((ts, d_tok), F32),
            pltpu.VMEM((ts, d_tok), BF16),
            pltpu.VMEM((ts, d_tok), BF16),
            pltpu.VMEM((ts, d_tok), BF16),
            pltpu.VMEM((len(LEVELS), ts, d_tok), BF16),
            pltpu.VMEM((ts, n_heads * DIAG * HG_EXPAND), BF16),
            pltpu.VMEM((n_heads, ts, LANES), F32),
            pltpu.VMEM((SUBLANES, d_tok), F32),
            pltpu.VMEM((ts, d_tok), F32),
            pltpu.VMEM((ts, d_tok), F32),
            pltpu.VMEM((ts, d_mem), BF16),
            pltpu.VMEM((ts, D), F32),
        ],
        compiler_params=pltpu.CompilerParams(
            dimension_semantics=("arbitrary", "arbitrary"), vmem_limit_bytes=VMEM_LIMIT),
        name="hgrn_layer",
    )(x, mem, gmix.reshape(1, D), gmem.reshape(1, D), wkv, win, lb, onorm.reshape(1, d_tok), wout,
      tri, esel, masks)


def _gmlp_layer(x, mem, gmix, gmem, wkv, win, lng, lnb, ws, bs, wout):
    B, S, D = x.shape
    M = mem.shape[1]
    d_tok = lng.shape[0]
    d_mem = D - d_tok
    n_groups = ws.shape[0]
    gdim = d_tok // n_groups
    ts = min(SEQ_TILE, S)
    assert S % ts == 0 and ts % GM_CHUNK == 0 and ws.shape[1:] == (GM_CHUNK, GM_CHUNK)
    bias = jnp.repeat(bs.astype(F32).T, gdim, axis=1)

    x_spec = pl.BlockSpec((None, ts, D), lambda b, j: (b, j, 0))
    mem_spec = pl.BlockSpec((None, M, D), lambda b, j: (b, 0, 0))
    return pl.pallas_call(
        _gmlp_layer_kernel,
        grid=(B, S // ts),
        in_specs=[x_spec, mem_spec, _const_spec((1, D)), _const_spec((1, D)), _const_spec(wkv.shape),
                  _const_spec(win.shape), _const_spec((1, d_tok)), _const_spec((1, d_tok)),
                  _const_spec(ws.shape), _const_spec(bias.shape), _const_spec(wout.shape)],
        out_specs=x_spec,
        out_shape=jax.ShapeDtypeStruct(x.shape, F32),
        scratch_shapes=[
            pltpu.VMEM((d_mem, MEM_HEADS * M), BF16),
            pltpu.VMEM((MEM_HEADS * M, d_mem), BF16),
            pltpu.VMEM((ts, d_tok), BF16),
            pltpu.VMEM((ts, D), BF16),
        ],
        compiler_params=pltpu.CompilerParams(
            dimension_semantics=("arbitrary", "arbitrary"), vmem_limit_bytes=VMEM_LIMIT),
        name="gmlp_layer",
    )(x, mem, gmix.reshape(1, D), gmem.reshape(1, D), wkv, win, lng.reshape(1, d_tok),
      lnb.reshape(1, d_tok), ws, bias, wout)


def _ffn_layer(x, g, w1, w2, gfin, final_norm):
    B, S, D = x.shape
    T = B * S
    tm = min(FFN_TILE, T)
    assert T % tm == 0
    x2 = x.reshape(T, D)
    row_spec = pl.BlockSpec((tm, D), lambda i: (i, 0))
    kernel = functools.partial(_ffn_kernel, final_norm=final_norm)
    out = pl.pallas_call(
        kernel,
        grid=(T // tm,),
        in_specs=[row_spec, _const_spec((1, D)), _const_spec(w1.shape), _const_spec(w2.shape),
                  _const_spec((1, D))],
        out_specs=row_spec,
        out_shape=jax.ShapeDtypeStruct((T, D), F32),
        compiler_params=pltpu.CompilerParams(
            dimension_semantics=("arbitrary",), vmem_limit_bytes=VMEM_LIMIT),
        name="ffn_layer",
    )(x2, g.reshape(1, D), w1, w2, gfin.reshape(1, D))
    return out.reshape(B, S, D)


def kernel(x, mem, mix_norm, mem_norm, w_mem_kv, w_out, hg_w_in, hg_lb, hg_onorm, gm_w_in, gm_ln_g,
           gm_ln_b, gm_ws, gm_bs, ffn_norm, w_ffn_in, w_ffn_out, final_norm):
    depth = mix_norm.shape[0]
    n_mixers = 2
    x = x.astype(F32)
    mem = mem.astype(F32)
    for i in range(depth):
        j = i // n_mixers
        wkv = w_mem_kv[i].astype(BF16)
        wout = w_out[i].astype(BF16)
        if i % n_mixers == 0:
            x = _hgrn_layer(x, mem, mix_norm[i], mem_norm[i], wkv, hg_w_in[j].astype(BF16),
                            hg_lb.astype(F32), hg_onorm[j], wout, layer_idx=i)
        else:
            x = _gmlp_layer(x, mem, mix_norm[i], mem_norm[i], wkv, gm_w_in[j].astype(BF16),
                            gm_ln_g[j], gm_ln_b[j], gm_ws[j], gm_bs[j], wout)
        x = _ffn_layer(x, ffn_norm[i], w_ffn_in[i].astype(BF16), w_ffn_out[i].astype(BF16),
                       final_norm, final_norm=(i == depth - 1))
    return x
```

```python
import functools

import jax
import jax.numpy as jnp
from jax import lax
from jax.experimental import pallas as pl
from jax.experimental.pallas import tpu as pltpu

F32 = jnp.float32
BF16 = jnp.bfloat16
EPS = 1e-6

LANES = 128
SUBLANES = 8
HG_EXPAND = 128
CHUNK = 64
LEVELS = (32, 16, 8)
DIAG = 8
DIRECT_MIN_B = -80.0
GM_CHUNK = 128
MEM_HEADS = 4
SEQ_TILE = 256
FFN_TILE = 512
VMEM_LIMIT = 56 * 1024 * 1024

_NT = (((1,), (1,)), ((), ()))
_TN = (((0,), (0,)), ((), ()))


def _dot(a, b):
    return jnp.dot(a, b, preferred_element_type=F32)


def _rmsnorm(x, g):
    return x * lax.rsqrt(jnp.mean(x * x, axis=-1, keepdims=True) + EPS) * g


def _mem_kv_init(mem_ref, gmem_ref, wkv_ref, kbd_s, vbd_s):
    d_mem = vbd_s.shape[1]
    hdim = d_mem // MEM_HEADS
    mlen = kbd_s.shape[1] // MEM_HEADS
    mn = _rmsnorm(mem_ref[...], gmem_ref[...])
    kv = _dot(mn.astype(BF16), wkv_ref[...])
    k = kv[:, :d_mem] * (hdim ** -0.5)
    v = kv[:, d_mem:]
    kt = k.T
    row_head = lax.broadcasted_iota(jnp.int32, kt.shape, 0) // hdim
    col_head = lax.broadcasted_iota(jnp.int32, v.shape, 1) // hdim
    for h in range(MEM_HEADS):
        kbd_s[:, h * mlen:(h + 1) * mlen] = jnp.where(row_head == h, kt, 0.0).astype(BF16)
        vbd_s[h * mlen:(h + 1) * mlen, :] = jnp.where(col_head == h, v, 0.0).astype(BF16)


def _mem_attention(qm, kbd_s, vbd_s):
    mlen = kbd_s.shape[1] // MEM_HEADS
    s_all = _dot(qm.astype(BF16), kbd_s[...])
    probs = []
    for h in range(MEM_HEADS):
        sh = s_all[:, h * mlen:(h + 1) * mlen]
        e = jnp.exp(sh - jnp.max(sh, axis=-1, keepdims=True))
        probs.append((e / jnp.sum(e, axis=-1, keepdims=True)).astype(BF16))
    return _dot(jnp.concatenate(probs, axis=1), vbd_s[...])


def _out_proj(parts, wout_ref):
    acc = None
    r0 = 0
    for part in parts:
        term = _dot(part, wout_ref[r0:r0 + part.shape[1], :])
        acc = term if acc is None else acc + term
        r0 += part.shape[1]
    return acc


def _chunk_cumsum(a):
    rows, d = a.shape
    row = lax.broadcasted_iota(jnp.int32, (SUBLANES, d), 0)
    outs = []
    carry = None
    for i in range(rows // SUBLANES):
        blk = a[i * SUBLANES:(i + 1) * SUBLANES, :]
        shift = 1
        while shift < SUBLANES:
            blk = blk + jnp.where(row >= shift, pltpu.roll(blk, shift, axis=0), 0.0)
            shift *= 2
        if i % (CHUNK // SUBLANES) != 0:
            blk = blk + carry
        carry = jnp.broadcast_to(blk[SUBLANES - 1:SUBLANES, :], (SUBLANES, d))
        outs.append(blk)
    return jnp.concatenate(outs, axis=0)


def _hgrn_layer_kernel(x_ref, mem_ref, gmix_ref, gmem_ref, wkv_ref, win_ref, lb_ref, onorm_ref,
                       wout_ref, esel_ref, masks_ref, out_ref,
                       skv_s, kbd_s, vbd_s, p_s, b_s, kk_s, v_s, qe_s, kd_s, xw_s, pd_s, rd_s,
                       ebl_s, o_s, gm_s, mo_s, xprev_s, *, layer_idx):
    ts = x_ref.shape[0]
    d_tok = b_s.shape[1]
    n_heads = d_tok // HG_EXPAND
    n_chunks = ts // CHUNK
    j = pl.program_id(1)
    n_tiles = pl.num_programs(1) - 1

    @pl.when(j == 0)
    def _():
        skv_s[...] = jnp.zeros_like(skv_s)
        _mem_kv_init(mem_ref, gmem_ref, wkv_ref, kbd_s, vbd_s)
        o_s[...] = jnp.zeros_like(o_s)
        gm_s[...] = jnp.zeros_like(gm_s)
        mo_s[...] = jnp.zeros_like(mo_s)
        xprev_s[...] = jnp.zeros_like(xprev_s)
        ebl_s[...] = jnp.zeros_like(ebl_s)

    x = x_ref[...]
    h = _rmsnorm(x, gmix_ref[...]).astype(BF16)
    p_s[...] = _dot(h, win_ref[...])

    heads = []
    for hd in range(n_heads):
        cols = slice(hd * HG_EXPAND, (hd + 1) * HG_EXPAND)
        oh = o_s[:, cols]
        on = oh * lax.rsqrt(jnp.mean(oh * oh, axis=-1, keepdims=True) + EPS)
        heads.append((on * gm_s[:, cols]).astype(BF16))
    parts = [jnp.concatenate(heads[i:i + 2], axis=1) for i in range(0, n_heads, 2)] + [mo_s[...]]
    out_ref[...] = xprev_s[...] + _out_proj(parts, wout_ref)
    xprev_s[...] = x

    lbl = lb_ref[...]
    lbe = jnp.exp(lbl - jnp.max(lbl, axis=0, keepdims=True))
    lbv = jnp.sum(lbe[:layer_idx + 1], axis=0, keepdims=True) / jnp.sum(lbe, axis=0, keepdims=True)

    f = lbv + (1.0 - lbv) * jax.nn.sigmoid(p_s[:, d_tok:2 * d_tok])
    kk_s[...] = 1.0 - f
    b = _chunk_cumsum(jnp.log(f))
    b_s[...] = b
    qe_s[...] = (p_s[:, :d_tok] * jnp.exp(b)).astype(BF16)
    v_s[...] = p_s[:, 2 * d_tok:3 * d_tok].astype(BF16)

    for c in range(n_chunks):
        rows = slice(c * CHUNK, (c + 1) * CHUNK)
        bl = b_s[(c + 1) * CHUNK - 1:(c + 1) * CHUNK, :]
        kd_s[rows, :] = (kk_s[rows, :] * jnp.exp(bl - b_s[rows, :])).astype(BF16)
        ebl_s[c:c + 1, :] = jnp.exp(bl)

    direct_ok = jnp.min(b) >= DIRECT_MIN_B

    @pl.when(j < n_tiles)
    def _():
        def mix(scores_fn):
            et = ebl_s[...].T
            for c in range(n_chunks):
                rows = slice(c * CHUNK, (c + 1) * CHUNK)
                for hd in range(n_heads):
                    cols = slice(hd * HG_EXPAND, (hd + 1) * HG_EXPAND)
                    a = scores_fn(hd, rows, cols).astype(BF16)
                    vc = v_s[rows, cols]
                    skv = skv_s[hd]
                    lhs = jnp.concatenate([qe_s[rows, cols], a], axis=1)
                    rhs = jnp.concatenate([skv.astype(BF16), vc], axis=0)
                    o_s[rows, cols] = _dot(lhs, rhs)
                    upd = lax.dot_general(kd_s[rows, cols], vc, _TN, preferred_element_type=F32)
                    ecol = jnp.broadcast_to(et[hd * HG_EXPAND:(hd + 1) * HG_EXPAND, c:c + 1],
                                            (HG_EXPAND, HG_EXPAND))
                    skv_s[hd] = skv * ecol + upd
            gate = p_s[:, 3 * d_tok:4 * d_tok]
            gm_s[...] = gate * jax.nn.sigmoid(gate) * onorm_ref[...]
            mo_s[...] = _mem_attention(p_s[:, 4 * d_tok:], kbd_s, vbd_s).astype(BF16)

        @pl.when(direct_ok)
        def _():
            kx_s = xw_s.at[0]
            kx_s[...] = (kk_s[...] * jnp.exp(-b_s[...])).astype(BF16)
            causal = masks_ref[len(LEVELS) + 1] != 0.0

            def scores(hd, rows, cols):
                s = lax.dot_general(qe_s[rows, cols], kx_s[rows, cols], _NT, preferred_element_type=F32)
                return jnp.where(causal, s, 0.0)

            mix(scores)

        @pl.when(jnp.logical_not(direct_ok))
        def _():
            for li, w in enumerate(LEVELS):
                for g0 in range(0, ts, 2 * w):
                    m = b_s[g0 + w:g0 + w + 1, :]
                    xk = kk_s[g0:g0 + w, :] * jnp.exp(m - b_s[g0:g0 + w, :])
                    xq = p_s[g0 + w:g0 + 2 * w, :d_tok] * jnp.exp(b_s[g0 + w:g0 + 2 * w, :] - m)
                    xw_s[li, g0:g0 + 2 * w, :] = jnp.concatenate([xk, xq], axis=0).astype(BF16)

            def pd_body(i, carry):
                r0 = pl.multiple_of(i * 2 * DIAG, 2 * DIAG)
                q16 = p_s[pl.ds(r0, 2 * DIAG), :d_tok]
                b16 = b_s[pl.ds(r0, 2 * DIAG), :]
                for c in range(DIAG):
                    def partner(ref):
                        top = jnp.broadcast_to(ref[pl.ds(r0 + c, 1), :], (DIAG, d_tok))
                        bot = jnp.broadcast_to(ref[pl.ds(r0 + DIAG + c, 1), :], (DIAG, d_tok))
                        return jnp.concatenate([top, bot], axis=0)
                    val = q16 * partner(kk_s) * jnp.exp(jnp.minimum(b16 - partner(b_s), 0.0))
                    vb = val.astype(BF16)
                    for hd in range(n_heads):
                        col = (hd * DIAG + c) * HG_EXPAND
                        pd_s[pl.ds(r0, 2 * DIAG), col:col + HG_EXPAND] = vb[:, hd * HG_EXPAND:(hd + 1) * HG_EXPAND]
                return carry

            lax.fori_loop(0, ts // (2 * DIAG), pd_body, 0)
            kdiag = DIAG * HG_EXPAND
            for hd in range(n_heads):
                rd_s[hd] = _dot(pd_s[:, hd * kdiag:(hd + 1) * kdiag], esel_ref[...])

            def scores(hd, rows, cols):
                a = rd_s[hd, rows, :CHUNK] * masks_ref[len(LEVELS)]
                for li in range(len(LEVELS)):
                    xl = xw_s[li, rows, cols]
                    a = a + lax.dot_general(xl, xl, _NT, preferred_element_type=F32) * masks_ref[li]
                return a

            mix(scores)


def _gmlp_layer_kernel(x_ref, mem_ref, gmix_ref, gmem_ref, wkv_ref, win_ref, lng_ref, lnb_ref,
                       ws_ref, bias_ref, wout_ref, out_ref, kbd_s, vbd_s, p_s, vn_s, xprev_s):
    ts = x_ref.shape[0]
    d_tok = vn_s.shape[1]
    n_groups = ws_ref.shape[0]
    gdim = d_tok // n_groups
    pp = p_s.at[1]

    @pl.when(pl.program_id(1) == 0)
    def _():
        _mem_kv_init(mem_ref, gmem_ref, wkv_ref, kbd_s, vbd_s)
        pp[...] = jnp.zeros(pp.shape, F32)
        xprev_s[...] = jnp.zeros_like(xprev_s)

    x = x_ref[...]
    h = _rmsnorm(x, gmix_ref[...]).astype(BF16)
    p_s[0] = _dot(h, win_ref[...])

    pz = pp[:, :2 * d_tok]
    z = 0.5 * pz * (1.0 + lax.erf(pz * (2.0 ** -0.5)))
    u = z[:, :d_tok]
    v = z[:, d_tok:]
    mu = jnp.mean(v, axis=-1, keepdims=True)
    vc = v - mu
    var = jnp.mean(vc * vc, axis=-1, keepdims=True)
    vn_s[...] = (vc * lax.rsqrt(var + EPS) * lng_ref[...] + lnb_ref[...]).astype(BF16)

    tril = (lax.broadcasted_iota(jnp.int32, (GM_CHUNK, GM_CHUNK), 0)
            >= lax.broadcasted_iota(jnp.int32, (GM_CHUNK, GM_CHUNK), 1))
    heads = []
    for g in range(n_groups):
        cols = slice(g * gdim, (g + 1) * gdim)
        wg = jnp.where(tril, ws_ref[g], 0.0).astype(BF16)
        sv = [_dot(wg, vn_s[n * GM_CHUNK:(n + 1) * GM_CHUNK, cols]) + bias_ref[:, cols]
              for n in range(ts // GM_CHUNK)]
        heads.append((u[:, cols] * jnp.concatenate(sv, axis=0)).astype(BF16))
    heads.append(_mem_attention(pp[:, 2 * d_tok:], kbd_s, vbd_s).astype(BF16))
    out_ref[...] = xprev_s[...] + _dot(jnp.concatenate(heads, axis=1), wout_ref[...])
    xprev_s[...] = x
    pp[...] = p_s[0]


def _ffn_kernel(x_ref, g_ref, w1_ref, w2_ref, gfin_ref, out_ref, *, final_norm):
    d_ff = w2_ref.shape[0]
    x = x_ref[...]
    h = _rmsnorm(x, g_ref[...]).astype(BF16)
    a = _dot(h, w1_ref[...])
    gate = a[:, :d_ff]
    act = (gate * jax.nn.sigmoid(gate) * a[:, d_ff:]).astype(BF16)
    y = x + _dot(act, w2_ref[...])
    if final_norm:
        y = _rmsnorm(y, gfin_ref[...])
    out_ref[...] = y


def _const_spec(shape):
    nd = len(shape)
    return pl.BlockSpec(shape, lambda *_: (0,) * nd)


def _skewed_specs(D, M, ts, n_tiles):
    x_spec = pl.BlockSpec((None, ts, D), lambda b, j: (b, jnp.minimum(j, n_tiles - 1), 0))
    out_spec = pl.BlockSpec((None, ts, D), lambda b, j: (b, jnp.maximum(j - 1, 0), 0))
    mem_spec = pl.BlockSpec((None, M, D), lambda b, j: (b, 0, 0))
    return x_spec, out_spec, mem_spec


def _hgrn_layer(x, mem, gmix, gmem, wkv, win, lb, onorm, wout, layer_idx):
    B, S, D = x.shape
    M = mem.shape[1]
    d_tok = lb.shape[1]
    d_mem = D - d_tok
    n_heads = d_tok // HG_EXPAND
    ts = min(SEQ_TILE, S)
    n_tiles = S // ts
    assert S % ts == 0 and ts % CHUNK == 0 and d_tok % HG_EXPAND == 0 and ts // CHUNK <= SUBLANES

    kidx = jnp.arange(DIAG * HG_EXPAND) // HG_EXPAND
    esel = (kidx[:, None] == (jnp.arange(LANES)[None, :] % DIAG)).astype(BF16)
    t = jnp.arange(CHUNK)[:, None]
    s = jnp.arange(CHUNK)[None, :]
    masks = [(((t // w) % 2 == 1) & (s // w == t // w - 1)) for w in LEVELS]
    masks.append((s // DIAG == t // DIAG) & (s % DIAG <= t % DIAG))
    masks.append(s <= t)
    masks = jnp.stack(masks).astype(F32)

    x_spec, out_spec, mem_spec = _skewed_specs(D, M, ts, n_tiles)
    kernel = functools.partial(_hgrn_layer_kernel, layer_idx=layer_idx)
    return pl.pallas_call(
        kernel,
        grid=(B, n_tiles + 1),
        in_specs=[x_spec, mem_spec, _const_spec((1, D)), _const_spec((1, D)), _const_spec(wkv.shape),
                  _const_spec(win.shape), _const_spec(lb.shape), _const_spec((1, d_tok)),
                  _const_spec(wout.shape), _const_spec(esel.shape),
                  _const_spec(masks.shape)],
        out_specs=out_spec,
        out_shape=jax.ShapeDtypeStruct(x.shape, F32),
        scratch_shapes=[
            pltpu.VMEM((n_heads, HG_EXPAND, HG_EXPAND), F32),
            pltpu.VMEM((d_mem, MEM_HEADS * M), BF16),
            pltpu.VMEM((MEM_HEADS * M, d_mem), BF16),
            pltpu.VMEM((ts, win.shape[1]), F32),
            pltpu.VMEM((ts, d_tok), F32),
            pltpu.VMEM((ts, d_tok), F32),
            pltpu.VMEM((ts, d_tok), BF16),
            pltpu.VMEM((ts, d_tok), BF16),
            pltpu.VMEM((ts, d_tok), BF16),
            pltpu.VMEM((len(LEVELS), ts, d_tok), BF16),
            pltpu.VMEM((ts, n_heads * DIAG * HG_EXPAND), BF16),
            pltpu.VMEM((n_heads, ts, LANES), F32),
            pltpu.VMEM((SUBLANES, d_tok), F32),
            pltpu.VMEM((ts, d_tok), F32),
            pltpu.VMEM((ts, d_tok), F32),
            pltpu.VMEM((ts, d_mem), BF16),
            pltpu.VMEM((ts, D), F32),
        ],
        compiler_params=pltpu.CompilerParams(
            dimension_semantics=("arbitrary", "arbitrary"), vmem_limit_bytes=VMEM_LIMIT),
        name="hgrn_layer",
    )(x, mem, gmix.reshape(1, D), gmem.reshape(1, D), wkv, win, lb, onorm.reshape(1, d_tok), wout,
      esel, masks)


def _gmlp_layer(x, mem, gmix, gmem, wkv, win, lng, lnb, ws, bs, wout):
    B, S, D = x.shape
    M = mem.shape[1]
    d_tok = lng.shape[0]
    d_mem = D - d_tok
    n_groups = ws.shape[0]
    gdim = d_tok // n_groups
    ts = min(SEQ_TILE, S)
    n_tiles = S // ts
    assert S % ts == 0 and ts % GM_CHUNK == 0 and ws.shape[1:] == (GM_CHUNK, GM_CHUNK)
    bias = jnp.repeat(bs.astype(F32).T, gdim, axis=1)

    x_spec, out_spec, mem_spec = _skewed_specs(D, M, ts, n_tiles)
    return pl.pallas_call(
        _gmlp_layer_kernel,
        grid=(B, n_tiles + 1),
        in_specs=[x_spec, mem_spec, _const_spec((1, D)), _const_spec((1, D)), _const_spec(wkv.shape),
                  _const_spec(win.shape), _const_spec((1, d_tok)), _const_spec((1, d_tok)),
                  _const_spec(ws.shape), _const_spec(bias.shape), _const_spec(wout.shape)],
        out_specs=out_spec,
        out_shape=jax.ShapeDtypeStruct(x.shape, F32),
        scratch_shapes=[
            pltpu.VMEM((d_mem, MEM_HEADS * M), BF16),
            pltpu.VMEM((MEM_HEADS * M, d_mem), BF16),
            pltpu.VMEM((2, ts, win.shape[1]), F32),
            pltpu.VMEM((ts, d_tok), BF16),
            pltpu.VMEM((ts, D), F32),
        ],
        compiler_params=pltpu.CompilerParams(
            dimension_semantics=("arbitrary", "arbitrary"), vmem_limit_bytes=VMEM_LIMIT),
        name="gmlp_layer",
    )(x, mem, gmix.reshape(1, D), gmem.reshape(1, D), wkv, win, lng.reshape(1, d_tok),
      lnb.reshape(1, d_tok), ws, bias, wout)


def _ffn_layer(x, g, w1, w2, gfin, final_norm):
    B, S, D = x.shape
    T = B * S
    tm = min(FFN_TILE, T)
    assert T % tm == 0
    x2 = x.reshape(T, D)
    row_spec = pl.BlockSpec((tm, D), lambda i: (i, 0))
    kernel = functools.partial(_ffn_kernel, final_norm=final_norm)
    out = pl.pallas_call(
        kernel,
        grid=(T // tm,),
        in_specs=[row_spec, _const_spec((1, D)), _const_spec(w1.shape), _const_spec(w2.shape),
                  _const_spec((1, D))],
        out_specs=row_spec,
        out_shape=jax.ShapeDtypeStruct((T, D), F32),
        compiler_params=pltpu.CompilerParams(
            dimension_semantics=("arbitrary",), vmem_limit_bytes=VMEM_LIMIT),
        name="ffn_layer",
    )(x2, g.reshape(1, D), w1, w2, gfin.reshape(1, D))
    return out.reshape(B, S, D)


def kernel(x, mem, mix_norm, mem_norm, w_mem_kv, w_out, hg_w_in, hg_lb, hg_onorm, gm_w_in, gm_ln_g,
           gm_ln_b, gm_ws, gm_bs, ffn_norm, w_ffn_in, w_ffn_out, final_norm):
    depth = mix_norm.shape[0]
    n_mixers = 2
    x = x.astype(F32)
    mem = mem.astype(F32)
    for i in range(depth):
        j = i // n_mixers
        wkv = w_mem_kv[i].astype(BF16)
        wout = w_out[i].astype(BF16)
        if i % n_mixers == 0:
            x = _hgrn_layer(x, mem, mix_norm[i], mem_norm[i], wkv, hg_w_in[j].astype(BF16),
                            hg_lb.astype(F32), hg_onorm[j], wout, layer_idx=i)
        else:
            x = _gmlp_layer(x, mem, mix_norm[i], mem_norm[i], wkv, gm_w_in[j].astype(BF16),
                            gm_ln_g[j], gm_ln_b[j], gm_ws[j], gm_bs[j], wout)
        x = _ffn_layer(x, ffn_norm[i], w_ffn_in[i].astype(BF16), w_ffn_out[i].astype(BF16),
                       final_norm, final_norm=(i == depth - 1))
    return x
```

```python
import functools

import jax
import jax.numpy as jnp
from jax import lax
from jax.experimental import pallas as pl
from jax.experimental.pallas import tpu as pltpu

F32 = jnp.float32
BF16 = jnp.bfloat16
EPS = 1e-6

LANES = 128
SUBLANES = 8
HG_EXPAND = 128
CHUNK = 64
LEVELS = (32, 16, 8)
DIAG = 8
DIRECT_MIN_B = -80.0
GM_CHUNK = 128
MEM_HEADS = 4
SEQ_TILE = 256
FFN_TILE = 512
VMEM_LIMIT = 56 * 1024 * 1024

_NT = (((1,), (1,)), ((), ()))
_TN = (((0,), (0,)), ((), ()))


def _dot(a, b):
    return jnp.dot(a, b, preferred_element_type=F32)


def _rmsnorm(x, g):
    return x * lax.rsqrt(jnp.mean(x * x, axis=-1, keepdims=True) + EPS) * g


def _mem_kv_init(mem_ref, gmem_ref, wkv_ref, kbd_s, vbd_s):
    d_mem = vbd_s.shape[1]
    hdim = d_mem // MEM_HEADS
    mlen = kbd_s.shape[1] // MEM_HEADS
    mn = _rmsnorm(mem_ref[...], gmem_ref[...])
    kv = _dot(mn.astype(BF16), wkv_ref[...])
    k = kv[:, :d_mem] * (hdim ** -0.5)
    v = kv[:, d_mem:]
    kt = k.T
    row_head = lax.broadcasted_iota(jnp.int32, kt.shape, 0) // hdim
    col_head = lax.broadcasted_iota(jnp.int32, v.shape, 1) // hdim
    for h in range(MEM_HEADS):
        kbd_s[:, h * mlen:(h + 1) * mlen] = jnp.where(row_head == h, kt, 0.0).astype(BF16)
        vbd_s[h * mlen:(h + 1) * mlen, :] = jnp.where(col_head == h, v, 0.0).astype(BF16)


def _mem_attention(qm, kbd_s, vbd_s):
    mlen = kbd_s.shape[1] // MEM_HEADS
    s_all = _dot(qm.astype(BF16), kbd_s[...])
    probs = []
    for h in range(MEM_HEADS):
        sh = s_all[:, h * mlen:(h + 1) * mlen]
        e = jnp.exp(sh - jnp.max(sh, axis=-1, keepdims=True))
        probs.append((e / jnp.sum(e, axis=-1, keepdims=True)).astype(BF16))
    return _dot(jnp.concatenate(probs, axis=1), vbd_s[...])


def _out_proj(parts, wout_ref):
    acc = None
    r0 = 0
    for part in parts:
        term = _dot(part, wout_ref[r0:r0 + part.shape[1], :])
        acc = term if acc is None else acc + term
        r0 += part.shape[1]
    return acc


def _chunk_cumsum(a):
    rows, d = a.shape
    row = lax.broadcasted_iota(jnp.int32, (SUBLANES, d), 0)
    outs = []
    carry = None
    for i in range(rows // SUBLANES):
        blk = a[i * SUBLANES:(i + 1) * SUBLANES, :]
        shift = 1
        while shift < SUBLANES:
            blk = blk + jnp.where(row >= shift, pltpu.roll(blk, shift, axis=0), 0.0)
            shift *= 2
        if i % (CHUNK // SUBLANES) != 0:
            blk = blk + carry
        carry = jnp.broadcast_to(blk[SUBLANES - 1:SUBLANES, :], (SUBLANES, d))
        outs.append(blk)
    return jnp.concatenate(outs, axis=0)


def _hgrn_layer_kernel(x_ref, mem_ref, gmix_ref, gmem_ref, wkv_ref, win_ref, lb_ref, onorm_ref,
                       wout_ref, esel_ref, masks_ref, out_ref,
                       skv_s, kbd_s, vbd_s, p_s, b_s, kk_s, v_s, qe_s, kd_s, xw_s, pd_s, rd_s,
                       ebl_s, o_s, gm_s, mo_s, xprev_s, *, layer_idx, n_tiles):
    ts = x_ref.shape[0]
    d_tok = b_s.shape[1]
    n_heads = d_tok // HG_EXPAND
    n_chunks = ts // CHUNK
    i = pl.program_id(0)
    has_tile = i < pl.num_programs(0) - 1

    @pl.when(jnp.logical_and(i % n_tiles == 0, has_tile))
    def _():
        skv_s[...] = jnp.zeros_like(skv_s)
        _mem_kv_init(mem_ref, gmem_ref, wkv_ref, kbd_s, vbd_s)

    @pl.when(i == 0)
    def _():
        o_s[...] = jnp.zeros_like(o_s)
        gm_s[...] = jnp.zeros_like(gm_s)
        mo_s[...] = jnp.zeros_like(mo_s)
        xprev_s[...] = jnp.zeros_like(xprev_s)
        ebl_s[...] = jnp.zeros_like(ebl_s)

    x = x_ref[...]
    h = _rmsnorm(x, gmix_ref[...]).astype(BF16)
    p_s[...] = _dot(h, win_ref[...])

    heads = []
    for hd in range(n_heads):
        cols = slice(hd * HG_EXPAND, (hd + 1) * HG_EXPAND)
        oh = o_s[:, cols]
        on = oh * lax.rsqrt(jnp.mean(oh * oh, axis=-1, keepdims=True) + EPS)
        heads.append((on * gm_s[:, cols]).astype(BF16))
    parts = [jnp.concatenate(heads[k:k + 2], axis=1) for k in range(0, n_heads, 2)] + [mo_s[...]]
    out_ref[...] = xprev_s[...] + _out_proj(parts, wout_ref)
    xprev_s[...] = x

    lbl = lb_ref[...]
    lbe = jnp.exp(lbl - jnp.max(lbl, axis=0, keepdims=True))
    lbv = jnp.sum(lbe[:layer_idx + 1], axis=0, keepdims=True) / jnp.sum(lbe, axis=0, keepdims=True)

    f = lbv + (1.0 - lbv) * jax.nn.sigmoid(p_s[:, d_tok:2 * d_tok])
    kk_s[...] = 1.0 - f
    b = _chunk_cumsum(jnp.log(f))
    b_s[...] = b
    qe_s[...] = (p_s[:, :d_tok] * jnp.exp(b)).astype(BF16)
    v_s[...] = p_s[:, 2 * d_tok:3 * d_tok].astype(BF16)

    for c in range(n_chunks):
        rows = slice(c * CHUNK, (c + 1) * CHUNK)
        bl = b_s[(c + 1) * CHUNK - 1:(c + 1) * CHUNK, :]
        kd_s[rows, :] = (kk_s[rows, :] * jnp.exp(bl - b_s[rows, :])).astype(BF16)
        ebl_s[c:c + 1, :] = jnp.exp(bl)

    direct_ok = jnp.min(b) >= DIRECT_MIN_B

    @pl.when(has_tile)
    def _():
        def mix(scores_fn):
            et = ebl_s[...].T
            for c in range(n_chunks):
                rows = slice(c * CHUNK, (c + 1) * CHUNK)
                for hd in range(n_heads):
                    cols = slice(hd * HG_EXPAND, (hd + 1) * HG_EXPAND)
                    a = scores_fn(hd, rows, cols).astype(BF16)
                    vc = v_s[rows, cols]
                    skv = skv_s[hd]
                    lhs = jnp.concatenate([qe_s[rows, cols], a], axis=1)
                    rhs = jnp.concatenate([skv.astype(BF16), vc], axis=0)
                    o_s[rows, cols] = _dot(lhs, rhs)
                    upd = lax.dot_general(kd_s[rows, cols], vc, _TN, preferred_element_type=F32)
                    ecol = jnp.broadcast_to(et[hd * HG_EXPAND:(hd + 1) * HG_EXPAND, c:c + 1],
                                            (HG_EXPAND, HG_EXPAND))
                    skv_s[hd] = skv * ecol + upd
            gate = p_s[:, 3 * d_tok:4 * d_tok]
            gm_s[...] = gate * jax.nn.sigmoid(gate) * onorm_ref[...]
            mo_s[...] = _mem_attention(p_s[:, 4 * d_tok:], kbd_s, vbd_s).astype(BF16)

        @pl.when(direct_ok)
        def _():
            kx_s = xw_s.at[0]
            kx_s[...] = (kk_s[...] * jnp.exp(-b_s[...])).astype(BF16)
            causal = masks_ref[len(LEVELS) + 1] != 0.0

            def scores(hd, rows, cols):
                s = lax.dot_general(qe_s[rows, cols], kx_s[rows, cols], _NT, preferred_element_type=F32)
                return jnp.where(causal, s, 0.0)

            mix(scores)

        @pl.when(jnp.logical_not(direct_ok))
        def _():
            for li, w in enumerate(LEVELS):
                for g0 in range(0, ts, 2 * w):
                    m = b_s[g0 + w:g0 + w + 1, :]
                    xk = kk_s[g0:g0 + w, :] * jnp.exp(m - b_s[g0:g0 + w, :])
                    xq = p_s[g0 + w:g0 + 2 * w, :d_tok] * jnp.exp(b_s[g0 + w:g0 + 2 * w, :] - m)
                    xw_s[li, g0:g0 + 2 * w, :] = jnp.concatenate([xk, xq], axis=0).astype(BF16)

            def pd_body(i, carry):
                r0 = pl.multiple_of(i * 2 * DIAG, 2 * DIAG)
                q16 = p_s[pl.ds(r0, 2 * DIAG), :d_tok]
                b16 = b_s[pl.ds(r0, 2 * DIAG), :]
                for c in range(DIAG):
                    def partner(ref):
                        top = jnp.broadcast_to(ref[pl.ds(r0 + c, 1), :], (DIAG, d_tok))
                        bot = jnp.broadcast_to(ref[pl.ds(r0 + DIAG + c, 1), :], (DIAG, d_tok))
                        return jnp.concatenate([top, bot], axis=0)
                    val = q16 * partner(kk_s) * jnp.exp(jnp.minimum(b16 - partner(b_s), 0.0))
                    vb = val.astype(BF16)
                    for hd in range(n_heads):
                        col = (hd * DIAG + c) * HG_EXPAND
                        pd_s[pl.ds(r0, 2 * DIAG), col:col + HG_EXPAND] = vb[:, hd * HG_EXPAND:(hd + 1) * HG_EXPAND]
                return carry

            lax.fori_loop(0, ts // (2 * DIAG), pd_body, 0)
            kdiag = DIAG * HG_EXPAND
            for hd in range(n_heads):
                rd_s[hd] = _dot(pd_s[:, hd * kdiag:(hd + 1) * kdiag], esel_ref[...])

            def scores(hd, rows, cols):
                a = rd_s[hd, rows, :CHUNK] * masks_ref[len(LEVELS)]
                for li in range(len(LEVELS)):
                    xl = xw_s[li, rows, cols]
                    a = a + lax.dot_general(xl, xl, _NT, preferred_element_type=F32) * masks_ref[li]
                return a

            mix(scores)


def _gmlp_layer_kernel(x_ref, mem_ref, gmix_ref, gmem_ref, wkv_ref, win_ref, lng_ref, lnb_ref,
                       ws_ref, bias_ref, wout_ref, out_ref, kbd_s, vbd_s, p_s, vn_s, xprev_s, *, n_tiles):
    ts = x_ref.shape[0]
    d_tok = vn_s.shape[1]
    n_groups = ws_ref.shape[0]
    gdim = d_tok // n_groups
    pp = p_s.at[1]

    i = pl.program_id(0)

    @pl.when(i == 0)
    def _():
        pp[...] = jnp.zeros(pp.shape, F32)
        xprev_s[...] = jnp.zeros_like(xprev_s)
        kbd_s[...] = jnp.zeros_like(kbd_s)
        vbd_s[...] = jnp.zeros_like(vbd_s)

    x = x_ref[...]
    h = _rmsnorm(x, gmix_ref[...]).astype(BF16)
    p_s[0] = _dot(h, win_ref[...])

    pz = pp[:, :2 * d_tok]
    z = 0.5 * pz * (1.0 + lax.erf(pz * (2.0 ** -0.5)))
    u = z[:, :d_tok]
    v = z[:, d_tok:]
    mu = jnp.mean(v, axis=-1, keepdims=True)
    vc = v - mu
    var = jnp.mean(vc * vc, axis=-1, keepdims=True)
    vn_s[...] = (vc * lax.rsqrt(var + EPS) * lng_ref[...] + lnb_ref[...]).astype(BF16)

    tril = (lax.broadcasted_iota(jnp.int32, (GM_CHUNK, GM_CHUNK), 0)
            >= lax.broadcasted_iota(jnp.int32, (GM_CHUNK, GM_CHUNK), 1))
    heads = []
    for g in range(n_groups):
        cols = slice(g * gdim, (g + 1) * gdim)
        wg = jnp.where(tril, ws_ref[g], 0.0).astype(BF16)
        sv = [_dot(wg, vn_s[n * GM_CHUNK:(n + 1) * GM_CHUNK, cols]) + bias_ref[:, cols]
              for n in range(ts // GM_CHUNK)]
        heads.append((u[:, cols] * jnp.concatenate(sv, axis=0)).astype(BF16))
    heads.append(_mem_attention(pp[:, 2 * d_tok:], kbd_s, vbd_s).astype(BF16))
    out_ref[...] = xprev_s[...] + _dot(jnp.concatenate(heads, axis=1), wout_ref[...])
    xprev_s[...] = x
    pp[...] = p_s[0]

    @pl.when(jnp.logical_and(i % n_tiles == 0, i < pl.num_programs(0) - 1))
    def _():
        _mem_kv_init(mem_ref, gmem_ref, wkv_ref, kbd_s, vbd_s)


def _ffn_kernel(x_ref, g_ref, w1_ref, w2_ref, gfin_ref, out_ref, *, final_norm):
    d_ff = w2_ref.shape[0]
    x = x_ref[...]
    h = _rmsnorm(x, g_ref[...]).astype(BF16)
    a = _dot(h, w1_ref[...])
    gate = a[:, :d_ff]
    act = (gate * jax.nn.sigmoid(gate) * a[:, d_ff:]).astype(BF16)
    y = x + _dot(act, w2_ref[...])
    if final_norm:
        y = _rmsnorm(y, gfin_ref[...])
    out_ref[...] = y


def _const_spec(shape):
    nd = len(shape)
    return pl.BlockSpec(shape, lambda *_: (0,) * nd)


def _layer_spec(stacked, layer):
    nd = stacked.ndim - 1
    return pl.BlockSpec((None,) + stacked.shape[1:], lambda *_: (layer,) + (0,) * nd)


def _skewed_specs(D, M, ts, n_tiles, total):
    def tile(t):
        return (t // n_tiles, t % n_tiles, 0)
    x_spec = pl.BlockSpec((None, ts, D), lambda i: tile(jnp.minimum(i, total - 1)))
    out_spec = pl.BlockSpec((None, ts, D), lambda i: tile(jnp.maximum(i - 1, 0)))
    mem_spec = pl.BlockSpec((None, M, D), lambda i: (jnp.minimum(i, total - 1) // n_tiles, 0, 0))
    return x_spec, out_spec, mem_spec


def _hgrn_layer(x, mem, gmix, gmem, wkv, win, lb, onorm, wout, layer_idx, mixer_idx):
    B, S, D = x.shape
    M = mem.shape[1]
    d_tok = lb.shape[1]
    d_mem = D - d_tok
    n_heads = d_tok // HG_EXPAND
    ts = min(SEQ_TILE, S)
    n_tiles = S // ts
    assert S % ts == 0 and ts % CHUNK == 0 and d_tok % HG_EXPAND == 0 and ts // CHUNK <= SUBLANES

    kidx = jnp.arange(DIAG * HG_EXPAND) // HG_EXPAND
    esel = (kidx[:, None] == (jnp.arange(LANES)[None, :] % DIAG)).astype(BF16)
    t = jnp.arange(CHUNK)[:, None]
    s = jnp.arange(CHUNK)[None, :]
    masks = [(((t // w) % 2 == 1) & (s // w == t // w - 1)) for w in LEVELS]
    masks.append((s // DIAG == t // DIAG) & (s % DIAG <= t % DIAG))
    masks.append(s <= t)
    masks = jnp.stack(masks).astype(F32)

    total = B * n_tiles
    x_spec, out_spec, mem_spec = _skewed_specs(D, M, ts, n_tiles, total)
    kernel = functools.partial(_hgrn_layer_kernel, layer_idx=layer_idx, n_tiles=n_tiles)
    return pl.pallas_call(
        kernel,
        grid=(total + 1,),
        in_specs=[x_spec, mem_spec, _const_spec((1, D)), _const_spec((1, D)), _layer_spec(wkv, layer_idx),
                  _layer_spec(win, mixer_idx), _const_spec(lb.shape), _const_spec((1, d_tok)),
                  _layer_spec(wout, layer_idx), _const_spec(esel.shape),
                  _const_spec(masks.shape)],
        out_specs=out_spec,
        out_shape=jax.ShapeDtypeStruct(x.shape, F32),
        scratch_shapes=[
            pltpu.VMEM((n_heads, HG_EXPAND, HG_EXPAND), F32),
            pltpu.VMEM((d_mem, MEM_HEADS * M), BF16),
            pltpu.VMEM((MEM_HEADS * M, d_mem), BF16),
            pltpu.VMEM((ts, win.shape[2]), F32),
            pltpu.VMEM((ts, d_tok), F32),
            pltpu.VMEM((ts, d_tok), F32),
            pltpu.VMEM((ts, d_tok), BF16),
            pltpu.VMEM((ts, d_tok), BF16),
            pltpu.VMEM((ts, d_tok), BF16),
            pltpu.VMEM((len(LEVELS), ts, d_tok), BF16),
            pltpu.VMEM((ts, n_heads * DIAG * HG_EXPAND), BF16),
            pltpu.VMEM((n_heads, ts, LANES), F32),
            pltpu.VMEM((SUBLANES, d_tok), F32),
            pltpu.VMEM((ts, d_tok), F32),
            pltpu.VMEM((ts, d_tok), F32),
            pltpu.VMEM((ts, d_mem), BF16),
            pltpu.VMEM((ts, D), F32),
        ],
        compiler_params=pltpu.CompilerParams(
            dimension_semantics=("arbitrary",), vmem_limit_bytes=VMEM_LIMIT),
        name="hgrn_layer",
    )(x, mem, gmix.reshape(1, D), gmem.reshape(1, D), wkv, win, lb, onorm.reshape(1, d_tok), wout,
      esel, masks)


def _gmlp_layer(x, mem, gmix, gmem, wkv, win, lng, lnb, ws, bs, wout, layer_idx, mixer_idx):
    B, S, D = x.shape
    M = mem.shape[1]
    d_tok = lng.shape[0]
    d_mem = D - d_tok
    n_groups = ws.shape[0]
    gdim = d_tok // n_groups
    ts = min(SEQ_TILE, S)
    n_tiles = S // ts
    assert S % ts == 0 and ts % GM_CHUNK == 0 and ws.shape[1:] == (GM_CHUNK, GM_CHUNK)
    bias = jnp.repeat(bs.astype(F32).T, gdim, axis=1)

    total = B * n_tiles
    x_spec, out_spec, mem_spec = _skewed_specs(D, M, ts, n_tiles, total)
    return pl.pallas_call(
        functools.partial(_gmlp_layer_kernel, n_tiles=n_tiles),
        grid=(total + 1,),
        in_specs=[x_spec, mem_spec, _const_spec((1, D)), _const_spec((1, D)), _layer_spec(wkv, layer_idx),
                  _layer_spec(win, mixer_idx), _const_spec((1, d_tok)), _const_spec((1, d_tok)),
                  _const_spec(ws.shape), _const_spec(bias.shape), _layer_spec(wout, layer_idx)],
        out_specs=out_spec,
        out_shape=jax.ShapeDtypeStruct(x.shape, F32),
        scratch_shapes=[
            pltpu.VMEM((d_mem, MEM_HEADS * M), BF16),
            pltpu.VMEM((MEM_HEADS * M, d_mem), BF16),
            pltpu.VMEM((2, ts, win.shape[2]), F32),
            pltpu.VMEM((ts, d_tok), BF16),
            pltpu.VMEM((ts, D), F32),
        ],
        compiler_params=pltpu.CompilerParams(
            dimension_semantics=("arbitrary",), vmem_limit_bytes=VMEM_LIMIT),
        name="gmlp_layer",
    )(x, mem, gmix.reshape(1, D), gmem.reshape(1, D), wkv, win, lng.reshape(1, d_tok),
      lnb.reshape(1, d_tok), ws, bias, wout)


def _ffn_layer(x, g, w1, w2, gfin, layer_idx, final_norm):
    B, S, D = x.shape
    T = B * S
    tm = min(FFN_TILE, T)
    assert T % tm == 0
    x2 = x.reshape(T, D)
    row_spec = pl.BlockSpec((tm, D), lambda i: (i, 0))
    kernel = functools.partial(_ffn_kernel, final_norm=final_norm)
    out = pl.pallas_call(
        kernel,
        grid=(T // tm,),
        in_specs=[row_spec, _const_spec((1, D)), _layer_spec(w1, layer_idx), _layer_spec(w2, layer_idx),
                  _const_spec((1, D))],
        out_specs=row_spec,
        out_shape=jax.ShapeDtypeStruct((T, D), F32),
        compiler_params=pltpu.CompilerParams(
            dimension_semantics=("arbitrary",), vmem_limit_bytes=VMEM_LIMIT),
        name="ffn_layer",
    )(x2, g.reshape(1, D), w1, w2, gfin.reshape(1, D))
    return out.reshape(B, S, D)


def kernel(x, mem, mix_norm, mem_norm, w_mem_kv, w_out, hg_w_in, hg_lb, hg_onorm, gm_w_in, gm_ln_g,
           gm_ln_b, gm_ws, gm_bs, ffn_norm, w_ffn_in, w_ffn_out, final_norm):
    depth = mix_norm.shape[0]
    n_mixers = 2
    x = x.astype(F32)
    mem = mem.astype(F32)
    wkv, wout, hg_win, gm_win, w1, w2 = (w.astype(BF16) for w in (w_mem_kv, w_out, hg_w_in, gm_w_in,
                                                                  w_ffn_in, w_ffn_out))
    for i in range(depth):
        j = i // n_mixers
        if i % n_mixers == 0:
            x = _hgrn_layer(x, mem, mix_norm[i], mem_norm[i], wkv, hg_win, hg_lb.astype(F32), hg_onorm[j],
                            wout, layer_idx=i, mixer_idx=j)
        else:
            x = _gmlp_layer(x, mem, mix_norm[i], mem_norm[i], wkv, gm_win, gm_ln_g[j], gm_ln_b[j],
                            gm_ws[j], gm_bs[j], wout, layer_idx=i, mixer_idx=j)
        x = _ffn_layer(x, ffn_norm[i], w1, w2, final_norm, layer_idx=i, final_norm=(i == depth - 1))
    return x
```

```python
import functools

import jax
import jax.numpy as jnp
from jax import lax
from jax.experimental import pallas as pl
from jax.experimental.pallas import tpu as pltpu

F32 = jnp.float32
BF16 = jnp.bfloat16
EPS = 1e-6

LANES = 128
SUBLANES = 8
HG_EXPAND = 128
CHUNK = 64
LEVELS = (32, 16, 8)
DIAG = 8
DIRECT_MIN_B = -80.0
GM_CHUNK = 128
MEM_HEADS = 4
HG_TILE = 256
GM_TILE = 512
FFN_TILE = 512
VMEM_LIMIT = 56 * 1024 * 1024

_NT = (((1,), (1,)), ((), ()))
_TN = (((0,), (0,)), ((), ()))


def _dot(a, b):
    return jnp.dot(a, b, preferred_element_type=F32)


def _rmsnorm(x, g):
    return x * lax.rsqrt(jnp.mean(x * x, axis=-1, keepdims=True) + EPS) * g


def _mem_kv_init(mem_ref, gmem_ref, wkv_ref, kbd_s, vbd_s):
    d_mem = vbd_s.shape[1]
    hdim = d_mem // MEM_HEADS
    mlen = kbd_s.shape[1] // MEM_HEADS
    mn = _rmsnorm(mem_ref[...], gmem_ref[...])
    kv = _dot(mn.astype(BF16), wkv_ref[...])
    k = kv[:, :d_mem] * (hdim ** -0.5)
    v = kv[:, d_mem:]
    kt = k.T
    row_head = lax.broadcasted_iota(jnp.int32, kt.shape, 0) // hdim
    col_head = lax.broadcasted_iota(jnp.int32, v.shape, 1) // hdim
    for h in range(MEM_HEADS):
        kbd_s[:, h * mlen:(h + 1) * mlen] = jnp.where(row_head == h, kt, 0.0).astype(BF16)
        vbd_s[h * mlen:(h + 1) * mlen, :] = jnp.where(col_head == h, v, 0.0).astype(BF16)


def _mem_attention(qm, kbd_s, vbd_s):
    mlen = kbd_s.shape[1] // MEM_HEADS
    s_all = _dot(qm.astype(BF16), kbd_s[...])
    probs = []
    for h in range(MEM_HEADS):
        sh = s_all[:, h * mlen:(h + 1) * mlen]
        e = jnp.exp(sh - jnp.max(sh, axis=-1, keepdims=True))
        probs.append((e / jnp.sum(e, axis=-1, keepdims=True)).astype(BF16))
    return _dot(jnp.concatenate(probs, axis=1), vbd_s[...])


def _out_proj(parts, wout_ref):
    acc = None
    r0 = 0
    for part in parts:
        term = _dot(part, wout_ref[r0:r0 + part.shape[1], :])
        acc = term if acc is None else acc + term
        r0 += part.shape[1]
    return acc


def _chunk_cumsum(a):
    rows, d = a.shape
    row = lax.broadcasted_iota(jnp.int32, (SUBLANES, d), 0)
    outs = []
    carry = None
    for i in range(rows // SUBLANES):
        blk = a[i * SUBLANES:(i + 1) * SUBLANES, :]
        shift = 1
        while shift < SUBLANES:
            blk = blk + jnp.where(row >= shift, pltpu.roll(blk, shift, axis=0), 0.0)
            shift *= 2
        if i % (CHUNK // SUBLANES) != 0:
            blk = blk + carry
        carry = jnp.broadcast_to(blk[SUBLANES - 1:SUBLANES, :], (SUBLANES, d))
        outs.append(blk)
    return jnp.concatenate(outs, axis=0)


def _hgrn_layer_kernel(x_ref, mem_ref, gmix_ref, gmem_ref, wkv_ref, win_ref, lb_ref, onorm_ref,
                       wout_ref, esel_ref, masks_ref, out_ref,
                       skv_s, kbd_s, vbd_s, p_s, b_s, kk_s, v_s, qe_s, kd_s, xw_s, pd_s, rd_s,
                       ebl_s, o_s, gm_s, mo_s, xprev_s, *, layer_idx, n_tiles):
    ts = x_ref.shape[0]
    d_tok = b_s.shape[1]
    n_heads = d_tok // HG_EXPAND
    n_chunks = ts // CHUNK
    i = pl.program_id(0)
    has_tile = i < pl.num_programs(0) - 1

    @pl.when(jnp.logical_and(i % n_tiles == 0, has_tile))
    def _():
        skv_s[...] = jnp.zeros_like(skv_s)
        _mem_kv_init(mem_ref, gmem_ref, wkv_ref, kbd_s, vbd_s)

    @pl.when(i == 0)
    def _():
        o_s[...] = jnp.zeros_like(o_s)
        gm_s[...] = jnp.zeros_like(gm_s)
        mo_s[...] = jnp.zeros_like(mo_s)
        xprev_s[...] = jnp.zeros_like(xprev_s)
        ebl_s[...] = jnp.zeros_like(ebl_s)

    x = x_ref[...]
    h = _rmsnorm(x, gmix_ref[...]).astype(BF16)
    p_s[...] = _dot(h, win_ref[...])

    heads = []
    for hd in range(n_heads):
        cols = slice(hd * HG_EXPAND, (hd + 1) * HG_EXPAND)
        oh = o_s[:, cols]
        on = oh * lax.rsqrt(jnp.mean(oh * oh, axis=-1, keepdims=True) + EPS)
        heads.append((on * gm_s[:, cols]).astype(BF16))
    parts = [jnp.concatenate(heads[k:k + 2], axis=1) for k in range(0, n_heads, 2)] + [mo_s[...]]
    out_ref[...] = xprev_s[...] + _out_proj(parts, wout_ref)
    xprev_s[...] = x

    lbl = lb_ref[...]
    lbe = jnp.exp(lbl - jnp.max(lbl, axis=0, keepdims=True))
    lbv = jnp.sum(lbe[:layer_idx + 1], axis=0, keepdims=True) / jnp.sum(lbe, axis=0, keepdims=True)

    f = lbv + (1.0 - lbv) * jax.nn.sigmoid(p_s[:, d_tok:2 * d_tok])
    kk_s[...] = 1.0 - f
    b = _chunk_cumsum(jnp.log(f))
    b_s[...] = b
    qe_s[...] = (p_s[:, :d_tok] * jnp.exp(b)).astype(BF16)
    v_s[...] = p_s[:, 2 * d_tok:3 * d_tok].astype(BF16)

    for c in range(n_chunks):
        rows = slice(c * CHUNK, (c + 1) * CHUNK)
        bl = b_s[(c + 1) * CHUNK - 1:(c + 1) * CHUNK, :]
        kd_s[rows, :] = (kk_s[rows, :] * jnp.exp(bl - b_s[rows, :])).astype(BF16)
        ebl_s[c:c + 1, :] = jnp.exp(bl)

    direct_ok = jnp.min(b) >= DIRECT_MIN_B

    @pl.when(has_tile)
    def _():
        def mix(scores_fn):
            et = ebl_s[...].T
            for c in range(n_chunks):
                rows = slice(c * CHUNK, (c + 1) * CHUNK)
                for hd in range(n_heads):
                    cols = slice(hd * HG_EXPAND, (hd + 1) * HG_EXPAND)
                    a = scores_fn(hd, rows, cols).astype(BF16)
                    vc = v_s[rows, cols]
                    skv = skv_s[hd]
                    lhs = jnp.concatenate([qe_s[rows, cols], a], axis=1)
                    rhs = jnp.concatenate([skv.astype(BF16), vc], axis=0)
                    o_s[rows, cols] = _dot(lhs, rhs)
                    upd = lax.dot_general(kd_s[rows, cols], vc, _TN, preferred_element_type=F32)
                    ecol = jnp.broadcast_to(et[hd * HG_EXPAND:(hd + 1) * HG_EXPAND, c:c + 1],
                                            (HG_EXPAND, HG_EXPAND))
                    skv_s[hd] = skv * ecol + upd
            gate = p_s[:, 3 * d_tok:4 * d_tok]
            gm_s[...] = gate * jax.nn.sigmoid(gate) * onorm_ref[...]
            mo_s[...] = _mem_attention(p_s[:, 4 * d_tok:], kbd_s, vbd_s).astype(BF16)

        @pl.when(direct_ok)
        def _():
            kx_s = xw_s.at[0]
            kx_s[...] = (kk_s[...] * jnp.exp(-b_s[...])).astype(BF16)
            causal = masks_ref[len(LEVELS) + 1] != 0.0

            def scores(hd, rows, cols):
                s = lax.dot_general(qe_s[rows, cols], kx_s[rows, cols], _NT, preferred_element_type=F32)
                return jnp.where(causal, s, 0.0)

            mix(scores)

        @pl.when(jnp.logical_not(direct_ok))
        def _():
            for li, w in enumerate(LEVELS):
                for g0 in range(0, ts, 2 * w):
                    m = b_s[g0 + w:g0 + w + 1, :]
                    xk = kk_s[g0:g0 + w, :] * jnp.exp(m - b_s[g0:g0 + w, :])
                    xq = p_s[g0 + w:g0 + 2 * w, :d_tok] * jnp.exp(b_s[g0 + w:g0 + 2 * w, :] - m)
                    xw_s[li, g0:g0 + 2 * w, :] = jnp.concatenate([xk, xq], axis=0).astype(BF16)

            def pd_body(i, carry):
                r0 = pl.multiple_of(i * 2 * DIAG, 2 * DIAG)
                q16 = p_s[pl.ds(r0, 2 * DIAG), :d_tok]
                b16 = b_s[pl.ds(r0, 2 * DIAG), :]
                for c in range(DIAG):
                    def partner(ref):
                        top = jnp.broadcast_to(ref[pl.ds(r0 + c, 1), :], (DIAG, d_tok))
                        bot = jnp.broadcast_to(ref[pl.ds(r0 + DIAG + c, 1), :], (DIAG, d_tok))
                        return jnp.concatenate([top, bot], axis=0)
                    val = q16 * partner(kk_s) * jnp.exp(jnp.minimum(b16 - partner(b_s), 0.0))
                    vb = val.astype(BF16)
                    for hd in range(n_heads):
                        col = (hd * DIAG + c) * HG_EXPAND
                        pd_s[pl.ds(r0, 2 * DIAG), col:col + HG_EXPAND] = vb[:, hd * HG_EXPAND:(hd + 1) * HG_EXPAND]
                return carry

            lax.fori_loop(0, ts // (2 * DIAG), pd_body, 0)
            kdiag = DIAG * HG_EXPAND
            for hd in range(n_heads):
                rd_s[hd] = _dot(pd_s[:, hd * kdiag:(hd + 1) * kdiag], esel_ref[...])

            def scores(hd, rows, cols):
                a = rd_s[hd, rows, :CHUNK] * masks_ref[len(LEVELS)]
                for li in range(len(LEVELS)):
                    xl = xw_s[li, rows, cols]
                    a = a + lax.dot_general(xl, xl, _NT, preferred_element_type=F32) * masks_ref[li]
                return a

            mix(scores)


def _gmlp_layer_kernel(x_ref, mem_ref, gmix_ref, gmem_ref, wkv_ref, win_ref, lng_ref, lnb_ref,
                       ws_ref, bias_ref, wout_ref, out_ref, kbd_s, vbd_s, p_s, vn_s, xprev_s, *, n_tiles):
    ts = x_ref.shape[0]
    d_tok = vn_s.shape[1]
    n_groups = ws_ref.shape[0]
    gdim = d_tok // n_groups
    pp = p_s.at[1]

    i = pl.program_id(0)

    @pl.when(i == 0)
    def _():
        pp[...] = jnp.zeros(pp.shape, F32)
        xprev_s[...] = jnp.zeros_like(xprev_s)
        kbd_s[...] = jnp.zeros_like(kbd_s)
        vbd_s[...] = jnp.zeros_like(vbd_s)

    x = x_ref[...]
    h = _rmsnorm(x, gmix_ref[...]).astype(BF16)
    p_s[0] = _dot(h, win_ref[...])

    pz = pp[:, :2 * d_tok]
    z = 0.5 * pz * (1.0 + lax.erf(pz * (2.0 ** -0.5)))
    u = z[:, :d_tok]
    v = z[:, d_tok:]
    mu = jnp.mean(v, axis=-1, keepdims=True)
    vc = v - mu
    var = jnp.mean(vc * vc, axis=-1, keepdims=True)
    vn_s[...] = (vc * lax.rsqrt(var + EPS) * lng_ref[...] + lnb_ref[...]).astype(BF16)

    tril = (lax.broadcasted_iota(jnp.int32, (GM_CHUNK, GM_CHUNK), 0)
            >= lax.broadcasted_iota(jnp.int32, (GM_CHUNK, GM_CHUNK), 1))
    heads = []
    for g in range(n_groups):
        cols = slice(g * gdim, (g + 1) * gdim)
        wg = jnp.where(tril, ws_ref[g], 0.0).astype(BF16)
        sv = [_dot(wg, vn_s[n * GM_CHUNK:(n + 1) * GM_CHUNK, cols]) + bias_ref[:, cols]
              for n in range(ts // GM_CHUNK)]
        heads.append((u[:, cols] * jnp.concatenate(sv, axis=0)).astype(BF16))
    heads.append(_mem_attention(pp[:, 2 * d_tok:], kbd_s, vbd_s).astype(BF16))
    out_ref[...] = xprev_s[...] + _dot(jnp.concatenate(heads, axis=1), wout_ref[...])
    xprev_s[...] = x
    pp[...] = p_s[0]

    @pl.when(jnp.logical_and(i % n_tiles == 0, i < pl.num_programs(0) - 1))
    def _():
        _mem_kv_init(mem_ref, gmem_ref, wkv_ref, kbd_s, vbd_s)


def _ffn_kernel(x_ref, g_ref, w1_ref, w2_ref, gfin_ref, out_ref, *, final_norm):
    d_ff = w2_ref.shape[0]
    x = x_ref[...]
    h = _rmsnorm(x, g_ref[...]).astype(BF16)
    a = _dot(h, w1_ref[...])
    gate = a[:, :d_ff]
    act = (gate * jax.nn.sigmoid(gate) * a[:, d_ff:]).astype(BF16)
    y = x + _dot(act, w2_ref[...])
    if final_norm:
        y = _rmsnorm(y, gfin_ref[...])
    out_ref[...] = y


def _const_spec(shape):
    nd = len(shape)
    return pl.BlockSpec(shape, lambda *_: (0,) * nd)


def _layer_spec(stacked, layer):
    nd = stacked.ndim - 1
    return pl.BlockSpec((None,) + stacked.shape[1:], lambda *_: (layer,) + (0,) * nd,
                        pipeline_mode=pl.Buffered(1))


def _skewed_specs(D, M, ts, n_tiles, total):
    def tile(t):
        return (t // n_tiles, t % n_tiles, 0)
    x_spec = pl.BlockSpec((None, ts, D), lambda i: tile(jnp.minimum(i, total - 1)))
    out_spec = pl.BlockSpec((None, ts, D), lambda i: tile(jnp.maximum(i - 1, 0)))
    mem_spec = pl.BlockSpec((None, M, D), lambda i: (jnp.minimum(i, total - 1) // n_tiles, 0, 0))
    return x_spec, out_spec, mem_spec


def _hgrn_layer(x, mem, gmix, gmem, wkv, win, lb, onorm, wout, layer_idx, mixer_idx):
    B, S, D = x.shape
    M = mem.shape[1]
    d_tok = lb.shape[1]
    d_mem = D - d_tok
    n_heads = d_tok // HG_EXPAND
    ts = min(HG_TILE, S)
    n_tiles = S // ts
    assert S % ts == 0 and ts % CHUNK == 0 and d_tok % HG_EXPAND == 0 and ts // CHUNK <= SUBLANES

    kidx = jnp.arange(DIAG * HG_EXPAND) // HG_EXPAND
    esel = (kidx[:, None] == (jnp.arange(LANES)[None, :] % DIAG)).astype(BF16)
    t = jnp.arange(CHUNK)[:, None]
    s = jnp.arange(CHUNK)[None, :]
    masks = [(((t // w) % 2 == 1) & (s // w == t // w - 1)) for w in LEVELS]
    masks.append((s // DIAG == t // DIAG) & (s % DIAG <= t % DIAG))
    masks.append(s <= t)
    masks = jnp.stack(masks).astype(F32)

    total = B * n_tiles
    x_spec, out_spec, mem_spec = _skewed_specs(D, M, ts, n_tiles, total)
    kernel = functools.partial(_hgrn_layer_kernel, layer_idx=layer_idx, n_tiles=n_tiles)
    return pl.pallas_call(
        kernel,
        grid=(total + 1,),
        in_specs=[x_spec, mem_spec, _const_spec((1, D)), _const_spec((1, D)), _layer_spec(wkv, layer_idx),
                  _layer_spec(win, mixer_idx), _const_spec(lb.shape), _const_spec((1, d_tok)),
                  _layer_spec(wout, layer_idx), _const_spec(esel.shape),
                  _const_spec(masks.shape)],
        out_specs=out_spec,
        out_shape=jax.ShapeDtypeStruct(x.shape, F32),
        scratch_shapes=[
            pltpu.VMEM((n_heads, HG_EXPAND, HG_EXPAND), F32),
            pltpu.VMEM((d_mem, MEM_HEADS * M), BF16),
            pltpu.VMEM((MEM_HEADS * M, d_mem), BF16),
            pltpu.VMEM((ts, win.shape[2]), F32),
            pltpu.VMEM((ts, d_tok), F32),
            pltpu.VMEM((ts, d_tok), F32),
            pltpu.VMEM((ts, d_tok), BF16),
            pltpu.VMEM((ts, d_tok), BF16),
            pltpu.VMEM((ts, d_tok), BF16),
            pltpu.VMEM((len(LEVELS), ts, d_tok), BF16),
            pltpu.VMEM((ts, n_heads * DIAG * HG_EXPAND), BF16),
            pltpu.VMEM((n_heads, ts, LANES), F32),
            pltpu.VMEM((SUBLANES, d_tok), F32),
            pltpu.VMEM((ts, d_tok), F32),
            pltpu.VMEM((ts, d_tok), F32),
            pltpu.VMEM((ts, d_mem), BF16),
            pltpu.VMEM((ts, D), F32),
        ],
        compiler_params=pltpu.CompilerParams(
            dimension_semantics=("arbitrary",), vmem_limit_bytes=VMEM_LIMIT),
        name="hgrn_layer",
    )(x, mem, gmix.reshape(1, D), gmem.reshape(1, D), wkv, win, lb, onorm.reshape(1, d_tok), wout,
      esel, masks)


def _gmlp_layer(x, mem, gmix, gmem, wkv, win, lng, lnb, ws, bs, wout, layer_idx, mixer_idx):
    B, S, D = x.shape
    M = mem.shape[1]
    d_tok = lng.shape[0]
    d_mem = D - d_tok
    n_groups = ws.shape[0]
    gdim = d_tok // n_groups
    ts = min(GM_TILE, S)
    n_tiles = S // ts
    assert S % ts == 0 and ts % GM_CHUNK == 0 and ws.shape[1:] == (GM_CHUNK, GM_CHUNK)
    bias = jnp.repeat(bs.astype(F32).T, gdim, axis=1)

    total = B * n_tiles
    x_spec, out_spec, mem_spec = _skewed_specs(D, M, ts, n_tiles, total)
    return pl.pallas_call(
        functools.partial(_gmlp_layer_kernel, n_tiles=n_tiles),
        grid=(total + 1,),
        in_specs=[x_spec, mem_spec, _const_spec((1, D)), _const_spec((1, D)), _layer_spec(wkv, layer_idx),
                  _layer_spec(win, mixer_idx), _const_spec((1, d_tok)), _const_spec((1, d_tok)),
                  _const_spec(ws.shape), _const_spec(bias.shape), _layer_spec(wout, layer_idx)],
        out_specs=out_spec,
        out_shape=jax.ShapeDtypeStruct(x.shape, F32),
        scratch_shapes=[
            pltpu.VMEM((d_mem, MEM_HEADS * M), BF16),
            pltpu.VMEM((MEM_HEADS * M, d_mem), BF16),
            pltpu.VMEM((2, ts, win.shape[2]), F32),
            pltpu.VMEM((ts, d_tok), BF16),
            pltpu.VMEM((ts, D), F32),
        ],
        compiler_params=pltpu.CompilerParams(
            dimension_semantics=("arbitrary",), vmem_limit_bytes=VMEM_LIMIT),
        name="gmlp_layer",
    )(x, mem, gmix.reshape(1, D), gmem.reshape(1, D), wkv, win, lng.reshape(1, d_tok),
      lnb.reshape(1, d_tok), ws, bias, wout)


def _ffn_layer(x, g, w1, w2, gfin, layer_idx, final_norm):
    B, S, D = x.shape
    T = B * S
    tm = min(FFN_TILE, T)
    assert T % tm == 0
    x2 = x.reshape(T, D)
    row_spec = pl.BlockSpec((tm, D), lambda i: (i, 0))
    kernel = functools.partial(_ffn_kernel, final_norm=final_norm)
    out = pl.pallas_call(
        kernel,
        grid=(T // tm,),
        in_specs=[row_spec, _const_spec((1, D)), _layer_spec(w1, layer_idx), _layer_spec(w2, layer_idx),
                  _const_spec((1, D))],
        out_specs=row_spec,
        out_shape=jax.ShapeDtypeStruct((T, D), F32),
        compiler_params=pltpu.CompilerParams(
            dimension_semantics=("arbitrary",), vmem_limit_bytes=VMEM_LIMIT),
        name="ffn_layer",
    )(x2, g.reshape(1, D), w1, w2, gfin.reshape(1, D))
    return out.reshape(B, S, D)


def kernel(x, mem, mix_norm, mem_norm, w_mem_kv, w_out, hg_w_in, hg_lb, hg_onorm, gm_w_in, gm_ln_g,
           gm_ln_b, gm_ws, gm_bs, ffn_norm, w_ffn_in, w_ffn_out, final_norm):
    depth = mix_norm.shape[0]
    n_mixers = 2
    x = x.astype(F32)
    mem = mem.astype(F32)
    wkv, wout, hg_win, gm_win, w1, w2 = (w.astype(BF16) for w in (w_mem_kv, w_out, hg_w_in, gm_w_in,
                                                                  w_ffn_in, w_ffn_out))
    for i in range(depth):
        j = i // n_mixers
        if i % n_mixers == 0:
            x = _hgrn_layer(x, mem, mix_norm[i], mem_norm[i], wkv, hg_win, hg_lb.astype(F32), hg_onorm[j],
                            wout, layer_idx=i, mixer_idx=j)
        else:
            x = _gmlp_layer(x, mem, mix_norm[i], mem_norm[i], wkv, gm_win, gm_ln_g[j], gm_ln_b[j],
                            gm_ws[j], gm_bs[j], wout, layer_idx=i, mixer_idx=j)
        x = _ffn_layer(x, ffn_norm[i], w1, w2, final_norm, layer_idx=i, final_norm=(i == depth - 1))
    return x
```

```python
import functools

import jax
import jax.numpy as jnp
from jax import lax
from jax.experimental import pallas as pl
from jax.experimental.pallas import tpu as pltpu

F32 = jnp.float32
BF16 = jnp.bfloat16
EPS = 1e-6

LANES = 128
SUBLANES = 8
HG_EXPAND = 128
CHUNK = 64
LEVELS = (32, 16, 8)
DIAG = 8
DIRECT_MIN_B = -80.0
GM_CHUNK = 128
MEM_HEADS = 4
HG_TILE = 256
GM_TILE = 512
FFN_TILE = 1024
FFN_CHUNK = 256
VMEM_LIMIT = 56 * 1024 * 1024

_NT = (((1,), (1,)), ((), ()))
_TN = (((0,), (0,)), ((), ()))


def _dot(a, b):
    return jnp.dot(a, b, preferred_element_type=F32)


def _rmsnorm(x, g):
    return x * lax.rsqrt(jnp.mean(x * x, axis=-1, keepdims=True) + EPS) * g


def _mem_kv_init(mem_ref, gmem_ref, wkv_ref, kbd_s, vbd_s):
    d_mem = vbd_s.shape[1]
    hdim = d_mem // MEM_HEADS
    mlen = kbd_s.shape[1] // MEM_HEADS
    mn = _rmsnorm(mem_ref[...], gmem_ref[...])
    kv = _dot(mn.astype(BF16), wkv_ref[...])
    k = kv[:, :d_mem] * (hdim ** -0.5)
    v = kv[:, d_mem:]
    kt = k.T
    row_head = lax.broadcasted_iota(jnp.int32, kt.shape, 0) // hdim
    col_head = lax.broadcasted_iota(jnp.int32, v.shape, 1) // hdim
    for h in range(MEM_HEADS):
        kbd_s[:, h * mlen:(h + 1) * mlen] = jnp.where(row_head == h, kt, 0.0).astype(BF16)
        vbd_s[h * mlen:(h + 1) * mlen, :] = jnp.where(col_head == h, v, 0.0).astype(BF16)


def _mem_attention(qm, kbd_s, vbd_s):
    mlen = kbd_s.shape[1] // MEM_HEADS
    s_all = _dot(qm.astype(BF16), kbd_s[...])
    probs = []
    for h in range(MEM_HEADS):
        sh = s_all[:, h * mlen:(h + 1) * mlen]
        e = jnp.exp(sh - jnp.max(sh, axis=-1, keepdims=True))
        probs.append((e / jnp.sum(e, axis=-1, keepdims=True)).astype(BF16))
    return _dot(jnp.concatenate(probs, axis=1), vbd_s[...])


def _out_proj(parts, wout_ref):
    acc = None
    r0 = 0
    for part in parts:
        term = _dot(part, wout_ref[r0:r0 + part.shape[1], :])
        acc = term if acc is None else acc + term
        r0 += part.shape[1]
    return acc


def _chunk_cumsum(a):
    rows, d = a.shape
    row = lax.broadcasted_iota(jnp.int32, (SUBLANES, d), 0)
    outs = []
    carry = None
    for i in range(rows // SUBLANES):
        blk = a[i * SUBLANES:(i + 1) * SUBLANES, :]
        shift = 1
        while shift < SUBLANES:
            blk = blk + jnp.where(row >= shift, pltpu.roll(blk, shift, axis=0), 0.0)
            shift *= 2
        if i % (CHUNK // SUBLANES) != 0:
            blk = blk + carry
        carry = jnp.broadcast_to(blk[SUBLANES - 1:SUBLANES, :], (SUBLANES, d))
        outs.append(blk)
    return jnp.concatenate(outs, axis=0)


def _hgrn_layer_kernel(x_ref, mem_ref, gmix_ref, gmem_ref, wkv_ref, win_ref, lb_ref, onorm_ref,
                       wout_ref, esel_ref, masks_ref, out_ref,
                       skv_s, kbd_s, vbd_s, p_s, b_s, kk_s, v_s, qe_s, kd_s, xw_s, pd_s, rd_s,
                       ebl_s, o_s, gm_s, mo_s, xprev_s, *, layer_idx, n_tiles):
    ts = x_ref.shape[0]
    d_tok = b_s.shape[1]
    n_heads = d_tok // HG_EXPAND
    n_chunks = ts // CHUNK
    i = pl.program_id(0)
    has_tile = i < pl.num_programs(0) - 1

    @pl.when(jnp.logical_and(i % n_tiles == 0, has_tile))
    def _():
        skv_s[...] = jnp.zeros_like(skv_s)
        _mem_kv_init(mem_ref, gmem_ref, wkv_ref, kbd_s, vbd_s)

    @pl.when(i == 0)
    def _():
        o_s[...] = jnp.zeros_like(o_s)
        gm_s[...] = jnp.zeros_like(gm_s)
        mo_s[...] = jnp.zeros_like(mo_s)
        xprev_s[...] = jnp.zeros_like(xprev_s)
        ebl_s[...] = jnp.zeros_like(ebl_s)

    x = x_ref[...]
    h = _rmsnorm(x, gmix_ref[...]).astype(BF16)
    p_s[...] = _dot(h, win_ref[...])

    heads = []
    for hd in range(n_heads):
        cols = slice(hd * HG_EXPAND, (hd + 1) * HG_EXPAND)
        oh = o_s[:, cols]
        on = oh * lax.rsqrt(jnp.mean(oh * oh, axis=-1, keepdims=True) + EPS)
        heads.append((on * gm_s[:, cols]).astype(BF16))
    parts = [jnp.concatenate(heads[k:k + 2], axis=1) for k in range(0, n_heads, 2)] + [mo_s[...]]
    out_ref[...] = xprev_s[...] + _out_proj(parts, wout_ref)
    xprev_s[...] = x

    mo_s[...] = _mem_attention(p_s[:, 4 * d_tok:], kbd_s, vbd_s).astype(BF16)

    lbl = lb_ref[...]
    lbe = jnp.exp(lbl - jnp.max(lbl, axis=0, keepdims=True))
    lbv = jnp.sum(lbe[:layer_idx + 1], axis=0, keepdims=True) / jnp.sum(lbe, axis=0, keepdims=True)

    f = lbv + (1.0 - lbv) * jax.nn.sigmoid(p_s[:, d_tok:2 * d_tok])
    kk_s[...] = 1.0 - f
    b = _chunk_cumsum(jnp.log(f))
    b_s[...] = b
    qe_s[...] = (p_s[:, :d_tok] * jnp.exp(b)).astype(BF16)
    v_s[...] = p_s[:, 2 * d_tok:3 * d_tok].astype(BF16)

    for c in range(n_chunks):
        rows = slice(c * CHUNK, (c + 1) * CHUNK)
        bl = b_s[(c + 1) * CHUNK - 1:(c + 1) * CHUNK, :]
        kd_s[rows, :] = (kk_s[rows, :] * jnp.exp(bl - b_s[rows, :])).astype(BF16)
        ebl_s[c:c + 1, :] = jnp.exp(bl)

    direct_ok = jnp.min(b) >= DIRECT_MIN_B

    @pl.when(has_tile)
    def _():
        def mix(scores_fn):
            et = ebl_s[...].T
            for c in range(n_chunks):
                rows = slice(c * CHUNK, (c + 1) * CHUNK)
                for hd in range(n_heads):
                    cols = slice(hd * HG_EXPAND, (hd + 1) * HG_EXPAND)
                    a = scores_fn(hd, rows, cols).astype(BF16)
                    vc = v_s[rows, cols]
                    skv = skv_s[hd]
                    lhs = jnp.concatenate([qe_s[rows, cols], a], axis=1)
                    rhs = jnp.concatenate([skv.astype(BF16), vc], axis=0)
                    o_s[rows, cols] = _dot(lhs, rhs)
                    upd = lax.dot_general(kd_s[rows, cols], vc, _TN, preferred_element_type=F32)
                    ecol = jnp.broadcast_to(et[hd * HG_EXPAND:(hd + 1) * HG_EXPAND, c:c + 1],
                                            (HG_EXPAND, HG_EXPAND))
                    skv_s[hd] = skv * ecol + upd
            gate = p_s[:, 3 * d_tok:4 * d_tok]
            gm_s[...] = gate * jax.nn.sigmoid(gate) * onorm_ref[...]

        @pl.when(direct_ok)
        def _():
            kx_s = xw_s.at[0]
            kx_s[...] = (kk_s[...] * jnp.exp(-b_s[...])).astype(BF16)
            causal = masks_ref[len(LEVELS) + 1] != 0.0

            def scores(hd, rows, cols):
                s = lax.dot_general(qe_s[rows, cols], kx_s[rows, cols], _NT, preferred_element_type=F32)
                return jnp.where(causal, s, 0.0)

            mix(scores)

        @pl.when(jnp.logical_not(direct_ok))
        def _():
            for li, w in enumerate(LEVELS):
                for g0 in range(0, ts, 2 * w):
                    m = b_s[g0 + w:g0 + w + 1, :]
                    xk = kk_s[g0:g0 + w, :] * jnp.exp(m - b_s[g0:g0 + w, :])
                    xq = p_s[g0 + w:g0 + 2 * w, :d_tok] * jnp.exp(b_s[g0 + w:g0 + 2 * w, :] - m)
                    xw_s[li, g0:g0 + 2 * w, :] = jnp.concatenate([xk, xq], axis=0).astype(BF16)

            def pd_body(i, carry):
                r0 = pl.multiple_of(i * 2 * DIAG, 2 * DIAG)
                q16 = p_s[pl.ds(r0, 2 * DIAG), :d_tok]
                b16 = b_s[pl.ds(r0, 2 * DIAG), :]
                for c in range(DIAG):
                    def partner(ref):
                        top = jnp.broadcast_to(ref[pl.ds(r0 + c, 1), :], (DIAG, d_tok))
                        bot = jnp.broadcast_to(ref[pl.ds(r0 + DIAG + c, 1), :], (DIAG, d_tok))
                        return jnp.concatenate([top, bot], axis=0)
                    val = q16 * partner(kk_s) * jnp.exp(jnp.minimum(b16 - partner(b_s), 0.0))
                    vb = val.astype(BF16)
                    for hd in range(n_heads):
                        col = (hd * DIAG + c) * HG_EXPAND
                        pd_s[pl.ds(r0, 2 * DIAG), col:col + HG_EXPAND] = vb[:, hd * HG_EXPAND:(hd + 1) * HG_EXPAND]
                return carry

            lax.fori_loop(0, ts // (2 * DIAG), pd_body, 0)
            kdiag = DIAG * HG_EXPAND
            for hd in range(n_heads):
                rd_s[hd] = _dot(pd_s[:, hd * kdiag:(hd + 1) * kdiag], esel_ref[...])

            def scores(hd, rows, cols):
                a = rd_s[hd, rows, :CHUNK] * masks_ref[len(LEVELS)]
                for li in range(len(LEVELS)):
                    xl = xw_s[li, rows, cols]
                    a = a + lax.dot_general(xl, xl, _NT, preferred_element_type=F32) * masks_ref[li]
                return a

            mix(scores)


def _gmlp_layer_kernel(x_ref, mem_ref, gmix_ref, gmem_ref, wkv_ref, win_ref, lng_ref, lnb_ref,
                       ws_ref, bias_ref, wout_ref, out_ref, kbd_s, vbd_s, p_s, vn_s, xprev_s, *, n_tiles):
    ts = x_ref.shape[0]
    d_tok = vn_s.shape[1]
    n_groups = ws_ref.shape[0]
    gdim = d_tok // n_groups
    pp = p_s.at[1]

    i = pl.program_id(0)

    @pl.when(i == 0)
    def _():
        pp[...] = jnp.zeros(pp.shape, F32)
        xprev_s[...] = jnp.zeros_like(xprev_s)
        kbd_s[...] = jnp.zeros_like(kbd_s)
        vbd_s[...] = jnp.zeros_like(vbd_s)

    x = x_ref[...]
    h = _rmsnorm(x, gmix_ref[...]).astype(BF16)
    p_s[0] = _dot(h, win_ref[...])

    pz = pp[:, :2 * d_tok]
    z = 0.5 * pz * (1.0 + lax.erf(pz * (2.0 ** -0.5)))
    u = z[:, :d_tok]
    v = z[:, d_tok:]
    mu = jnp.mean(v, axis=-1, keepdims=True)
    vc = v - mu
    var = jnp.mean(vc * vc, axis=-1, keepdims=True)
    vn_s[...] = (vc * lax.rsqrt(var + EPS) * lng_ref[...] + lnb_ref[...]).astype(BF16)

    tril = (lax.broadcasted_iota(jnp.int32, (GM_CHUNK, GM_CHUNK), 0)
            >= lax.broadcasted_iota(jnp.int32, (GM_CHUNK, GM_CHUNK), 1))
    heads = []
    for g in range(n_groups):
        cols = slice(g * gdim, (g + 1) * gdim)
        wg = jnp.where(tril, ws_ref[g], 0.0).astype(BF16)
        sv = [_dot(wg, vn_s[n * GM_CHUNK:(n + 1) * GM_CHUNK, cols]) + bias_ref[:, cols]
              for n in range(ts // GM_CHUNK)]
        heads.append((u[:, cols] * jnp.concatenate(sv, axis=0)).astype(BF16))
    heads.append(_mem_attention(pp[:, 2 * d_tok:], kbd_s, vbd_s).astype(BF16))
    out_ref[...] = xprev_s[...] + _dot(jnp.concatenate(heads, axis=1), wout_ref[...])
    xprev_s[...] = x
    pp[...] = p_s[0]

    @pl.when(jnp.logical_and(i % n_tiles == 0, i < pl.num_programs(0) - 1))
    def _():
        _mem_kv_init(mem_ref, gmem_ref, wkv_ref, kbd_s, vbd_s)


def _ffn_kernel(x_ref, g_ref, w1_ref, w2_ref, gfin_ref, out_ref, *, final_norm):
    d_ff = w2_ref.shape[0]
    x = x_ref[...]
    h = _rmsnorm(x, g_ref[...]).astype(BF16)
    y = x
    for c0 in range(0, d_ff, FFN_CHUNK):
        gate = _dot(h, w1_ref[:, c0:c0 + FFN_CHUNK])
        up = _dot(h, w1_ref[:, d_ff + c0:d_ff + c0 + FFN_CHUNK])
        act = (gate * jax.nn.sigmoid(gate) * up).astype(BF16)
        y = y + _dot(act, w2_ref[c0:c0 + FFN_CHUNK, :])
    if final_norm:
        y = _rmsnorm(y, gfin_ref[...])
    out_ref[...] = y


def _const_spec(shape):
    nd = len(shape)
    return pl.BlockSpec(shape, lambda *_: (0,) * nd)


def _layer_spec(stacked, layer):
    nd = stacked.ndim - 1
    return pl.BlockSpec((None,) + stacked.shape[1:], lambda *_: (layer,) + (0,) * nd,
                        pipeline_mode=pl.Buffered(1))


def _skewed_specs(D, M, ts, n_tiles, total):
    def tile(t):
        return (t // n_tiles, t % n_tiles, 0)
    x_spec = pl.BlockSpec((None, ts, D), lambda i: tile(jnp.minimum(i, total - 1)))
    out_spec = pl.BlockSpec((None, ts, D), lambda i: tile(jnp.maximum(i - 1, 0)))
    mem_spec = pl.BlockSpec((None, M, D), lambda i: (jnp.minimum(i, total - 1) // n_tiles, 0, 0))
    return x_spec, out_spec, mem_spec


def _hgrn_layer(x, mem, gmix, gmem, wkv, win, lb, onorm, wout, layer_idx, mixer_idx):
    B, S, D = x.shape
    M = mem.shape[1]
    d_tok = lb.shape[1]
    d_mem = D - d_tok
    n_heads = d_tok // HG_EXPAND
    ts = min(HG_TILE, S)
    n_tiles = S // ts
    assert S % ts == 0 and ts % CHUNK == 0 and d_tok % HG_EXPAND == 0 and ts // CHUNK <= SUBLANES

    kidx = jnp.arange(DIAG * HG_EXPAND) // HG_EXPAND
    esel = (kidx[:, None] == (jnp.arange(LANES)[None, :] % DIAG)).astype(BF16)
    t = jnp.arange(CHUNK)[:, None]
    s = jnp.arange(CHUNK)[None, :]
    masks = [(((t // w) % 2 == 1) & (s // w == t // w - 1)) for w in LEVELS]
    masks.append((s // DIAG == t // DIAG) & (s % DIAG <= t % DIAG))
    masks.append(s <= t)
    masks = jnp.stack(masks).astype(F32)

    total = B * n_tiles
    x_spec, out_spec, mem_spec = _skewed_specs(D, M, ts, n_tiles, total)
    kernel = functools.partial(_hgrn_layer_kernel, layer_idx=layer_idx, n_tiles=n_tiles)
    return pl.pallas_call(
        kernel,
        grid=(total + 1,),
        in_specs=[x_spec, mem_spec, _const_spec((1, D)), _const_spec((1, D)), _layer_spec(wkv, layer_idx),
                  _layer_spec(win, mixer_idx), _const_spec(lb.shape), _const_spec((1, d_tok)),
                  _layer_spec(wout, layer_idx), _const_spec(esel.shape),
                  _const_spec(masks.shape)],
        out_specs=out_spec,
        out_shape=jax.ShapeDtypeStruct(x.shape, F32),
        scratch_shapes=[
            pltpu.VMEM((n_heads, HG_EXPAND, HG_EXPAND), F32),
            pltpu.VMEM((d_mem, MEM_HEADS * M), BF16),
            pltpu.VMEM((MEM_HEADS * M, d_mem), BF16),
            pltpu.VMEM((ts, win.shape[2]), F32),
            pltpu.VMEM((ts, d_tok), F32),
            pltpu.VMEM((ts, d_tok), F32),
            pltpu.VMEM((ts, d_tok), BF16),
            pltpu.VMEM((ts, d_tok), BF16),
            pltpu.VMEM((ts, d_tok), BF16),
            pltpu.VMEM((len(LEVELS), ts, d_tok), BF16),
            pltpu.VMEM((ts, n_heads * DIAG * HG_EXPAND), BF16),
            pltpu.VMEM((n_heads, ts, LANES), F32),
            pltpu.VMEM((SUBLANES, d_tok), F32),
            pltpu.VMEM((ts, d_tok), F32),
            pltpu.VMEM((ts, d_tok), F32),
            pltpu.VMEM((ts, d_mem), BF16),
            pltpu.VMEM((ts, D), F32),
        ],
        compiler_params=pltpu.CompilerParams(
            dimension_semantics=("arbitrary",), vmem_limit_bytes=VMEM_LIMIT),
        name="hgrn_layer",
    )(x, mem, gmix.reshape(1, D), gmem.reshape(1, D), wkv, win, lb, onorm.reshape(1, d_tok), wout,
      esel, masks)


def _gmlp_layer(x, mem, gmix, gmem, wkv, win, lng, lnb, ws, bs, wout, layer_idx, mixer_idx):
    B, S, D = x.shape
    M = mem.shape[1]
    d_tok = lng.shape[0]
    d_mem = D - d_tok
    n_groups = ws.shape[0]
    gdim = d_tok // n_groups
    ts = min(GM_TILE, S)
    n_tiles = S // ts
    assert S % ts == 0 and ts % GM_CHUNK == 0 and ws.shape[1:] == (GM_CHUNK, GM_CHUNK)
    bias = jnp.repeat(bs.astype(F32).T, gdim, axis=1)

    total = B * n_tiles
    x_spec, out_spec, mem_spec = _skewed_specs(D, M, ts, n_tiles, total)
    return pl.pallas_call(
        functools.partial(_gmlp_layer_kernel, n_tiles=n_tiles),
        grid=(total + 1,),
        in_specs=[x_spec, mem_spec, _const_spec((1, D)), _const_spec((1, D)), _layer_spec(wkv, layer_idx),
                  _layer_spec(win, mixer_idx), _const_spec((1, d_tok)), _const_spec((1, d_tok)),
                  _const_spec(ws.shape), _const_spec(bias.shape), _layer_spec(wout, layer_idx)],
        out_specs=out_spec,
        out_shape=jax.ShapeDtypeStruct(x.shape, F32),
        scratch_shapes=[
            pltpu.VMEM((d_mem, MEM_HEADS * M), BF16),
            pltpu.VMEM((MEM_HEADS * M, d_mem), BF16),
            pltpu.VMEM((2, ts, win.shape[2]), F32),
            pltpu.VMEM((ts, d_tok), BF16),
            pltpu.VMEM((ts, D), F32),
        ],
        compiler_params=pltpu.CompilerParams(
            dimension_semantics=("arbitrary",), vmem_limit_bytes=VMEM_LIMIT),
        name="gmlp_layer",
    )(x, mem, gmix.reshape(1, D), gmem.reshape(1, D), wkv, win, lng.reshape(1, d_tok),
      lnb.reshape(1, d_tok), ws, bias, wout)


def _ffn_layer(x, g, w1, w2, gfin, layer_idx, final_norm):
    B, S, D = x.shape
    T = B * S
    tm = min(FFN_TILE, T)
    assert T % tm == 0 and w2.shape[1] % FFN_CHUNK == 0
    x2 = x.reshape(T, D)
    row_spec = pl.BlockSpec((tm, D), lambda i: (i, 0))
    kernel = functools.partial(_ffn_kernel, final_norm=final_norm)
    out = pl.pallas_call(
        kernel,
        grid=(T // tm,),
        in_specs=[row_spec, _const_spec((1, D)), _layer_spec(w1, layer_idx), _layer_spec(w2, layer_idx),
                  _const_spec((1, D))],
        out_specs=row_spec,
        out_shape=jax.ShapeDtypeStruct((T, D), F32),
        compiler_params=pltpu.CompilerParams(
            dimension_semantics=("arbitrary",), vmem_limit_bytes=VMEM_LIMIT),
        name="ffn_layer",
    )(x2, g.reshape(1, D), w1, w2, gfin.reshape(1, D))
    return out.reshape(B, S, D)


def kernel(x, mem, mix_norm, mem_norm, w_mem_kv, w_out, hg_w_in, hg_lb, hg_onorm, gm_w_in, gm_ln_g,
           gm_ln_b, gm_ws, gm_bs, ffn_norm, w_ffn_in, w_ffn_out, final_norm):
    depth = mix_norm.shape[0]
    n_mixers = 2
    x = x.astype(F32)
    mem = mem.astype(F32)
    wkv, wout, hg_win, gm_win, w1, w2 = (w.astype(BF16) for w in (w_mem_kv, w_out, hg_w_in, gm_w_in,
                                                                  w_ffn_in, w_ffn_out))
    for i in range(depth):
        j = i // n_mixers
        if i % n_mixers == 0:
            x = _hgrn_layer(x, mem, mix_norm[i], mem_norm[i], wkv, hg_win, hg_lb.astype(F32), hg_onorm[j],
                            wout, layer_idx=i, mixer_idx=j)
        else:
            x = _gmlp_layer(x, mem, mix_norm[i], mem_norm[i], wkv, gm_win, gm_ln_g[j], gm_ln_b[j],
                            gm_ws[j], gm_bs[j], wout, layer_idx=i, mixer_idx=j)
        x = _ffn_layer(x, ffn_norm[i], w1, w2, final_norm, layer_idx=i, final_norm=(i == depth - 1))
    return x
```

```python
import functools

import jax
import jax.numpy as jnp
from jax import lax
from jax.experimental import pallas as pl
from jax.experimental.pallas import tpu as pltpu

F32 = jnp.float32
BF16 = jnp.bfloat16
EPS = 1e-6

LANES = 128
SUBLANES = 8
HG_EXPAND = 128
CHUNK = 64
LEVELS = (32, 16, 8)
DIAG = 8
DIRECT_MIN_B = -80.0
GM_CHUNK = 128
MEM_HEADS = 4
PD_ROWS = 256
HG_TILE = 512
GM_TILE = 512
FFN_TILE = 1024
FFN_CHUNK = 256
VMEM_LIMIT = 56 * 1024 * 1024

_NT = (((1,), (1,)), ((), ()))
_TN = (((0,), (0,)), ((), ()))


def _dot(a, b):
    return jnp.dot(a, b, preferred_element_type=F32)


def _rmsnorm(x, g):
    return x * lax.rsqrt(jnp.mean(x * x, axis=-1, keepdims=True) + EPS) * g


def _mem_kv_init(mem_ref, gmem_ref, wkv_ref, kbd_s, vbd_s):
    d_mem = vbd_s.shape[1]
    hdim = d_mem // MEM_HEADS
    mlen = kbd_s.shape[1] // MEM_HEADS
    mn = _rmsnorm(mem_ref[...], gmem_ref[...])
    kv = _dot(mn.astype(BF16), wkv_ref[...])
    k = kv[:, :d_mem] * (hdim ** -0.5)
    v = kv[:, d_mem:]
    kt = k.T
    row_head = lax.broadcasted_iota(jnp.int32, kt.shape, 0) // hdim
    col_head = lax.broadcasted_iota(jnp.int32, v.shape, 1) // hdim
    for h in range(MEM_HEADS):
        kbd_s[:, h * mlen:(h + 1) * mlen] = jnp.where(row_head == h, kt, 0.0).astype(BF16)
        vbd_s[h * mlen:(h + 1) * mlen, :] = jnp.where(col_head == h, v, 0.0).astype(BF16)


def _mem_attention(qm, kbd_s, vbd_s):
    mlen = kbd_s.shape[1] // MEM_HEADS
    s_all = _dot(qm.astype(BF16), kbd_s[...])
    probs = []
    for h in range(MEM_HEADS):
        sh = s_all[:, h * mlen:(h + 1) * mlen]
        e = jnp.exp(sh - jnp.max(sh, axis=-1, keepdims=True))
        probs.append((e / jnp.sum(e, axis=-1, keepdims=True)).astype(BF16))
    return _dot(jnp.concatenate(probs, axis=1), vbd_s[...])


def _out_proj(parts, wout_ref):
    acc = None
    r0 = 0
    for part in parts:
        term = _dot(part, wout_ref[r0:r0 + part.shape[1], :])
        acc = term if acc is None else acc + term
        r0 += part.shape[1]
    return acc


def _chunk_cumsum(a):
    rows, d = a.shape
    row = lax.broadcasted_iota(jnp.int32, (SUBLANES, d), 0)
    outs = []
    carry = None
    for i in range(rows // SUBLANES):
        blk = a[i * SUBLANES:(i + 1) * SUBLANES, :]
        shift = 1
        while shift < SUBLANES:
            blk = blk + jnp.where(row >= shift, pltpu.roll(blk, shift, axis=0), 0.0)
            shift *= 2
        if i % (CHUNK // SUBLANES) != 0:
            blk = blk + carry
        carry = jnp.broadcast_to(blk[SUBLANES - 1:SUBLANES, :], (SUBLANES, d))
        outs.append(blk)
    return jnp.concatenate(outs, axis=0)


def _hgrn_layer_kernel(x_ref, mem_ref, gmix_ref, gmem_ref, wkv_ref, win_ref, lb_ref, onorm_ref,
                       wout_ref, esel_ref, masks_ref, out_ref,
                       skv_s, kbd_s, vbd_s, p_s, b_s, kk_s, v_s, qe_s, kd_s, xw_s, pd_s, rd_s,
                       ebl_s, o_s, gm_s, mo_s, xprev_s, *, layer_idx, n_tiles):
    ts = x_ref.shape[0]
    d_tok = b_s.shape[1]
    n_heads = d_tok // HG_EXPAND
    n_chunks = ts // CHUNK
    i = pl.program_id(0)
    has_tile = i < pl.num_programs(0) - 1

    @pl.when(jnp.logical_and(i % n_tiles == 0, has_tile))
    def _():
        skv_s[...] = jnp.zeros_like(skv_s)
        _mem_kv_init(mem_ref, gmem_ref, wkv_ref, kbd_s, vbd_s)

    @pl.when(i == 0)
    def _():
        o_s[...] = jnp.zeros_like(o_s)
        gm_s[...] = jnp.zeros_like(gm_s)
        mo_s[...] = jnp.zeros_like(mo_s)
        xprev_s[...] = jnp.zeros_like(xprev_s)
        ebl_s[...] = jnp.zeros_like(ebl_s)

    x = x_ref[...]
    h = _rmsnorm(x, gmix_ref[...]).astype(BF16)
    p_s[...] = _dot(h, win_ref[...])

    heads = []
    for hd in range(n_heads):
        cols = slice(hd * HG_EXPAND, (hd + 1) * HG_EXPAND)
        oh = o_s[:, cols]
        on = oh * lax.rsqrt(jnp.mean(oh * oh, axis=-1, keepdims=True) + EPS)
        heads.append((on * gm_s[:, cols]).astype(BF16))
    parts = [jnp.concatenate(heads[k:k + 2], axis=1) for k in range(0, n_heads, 2)] + [mo_s[...]]
    out_ref[...] = xprev_s[...] + _out_proj(parts, wout_ref)
    xprev_s[...] = x

    mo_s[...] = _mem_attention(p_s[:, 4 * d_tok:], kbd_s, vbd_s).astype(BF16)

    lbl = lb_ref[...]
    lbe = jnp.exp(lbl - jnp.max(lbl, axis=0, keepdims=True))
    lbv = jnp.sum(lbe[:layer_idx + 1], axis=0, keepdims=True) / jnp.sum(lbe, axis=0, keepdims=True)

    f = lbv + (1.0 - lbv) * jax.nn.sigmoid(p_s[:, d_tok:2 * d_tok])
    kk_s[...] = 1.0 - f
    b = _chunk_cumsum(jnp.log(f))
    b_s[...] = b
    qe_s[...] = (p_s[:, :d_tok] * jnp.exp(b)).astype(BF16)
    v_s[...] = p_s[:, 2 * d_tok:3 * d_tok].astype(BF16)

    for c in range(n_chunks):
        rows = slice(c * CHUNK, (c + 1) * CHUNK)
        bl = b_s[(c + 1) * CHUNK - 1:(c + 1) * CHUNK, :]
        kd_s[rows, :] = (kk_s[rows, :] * jnp.exp(bl - b_s[rows, :])).astype(BF16)
        ebl_s[c:c + 1, :] = jnp.exp(bl)

    direct_ok = jnp.min(b) >= DIRECT_MIN_B

    @pl.when(has_tile)
    def _():
        def mix(scores_fn):
            et = ebl_s[...].T
            for c in range(n_chunks):
                rows = slice(c * CHUNK, (c + 1) * CHUNK)
                for hd in range(n_heads):
                    cols = slice(hd * HG_EXPAND, (hd + 1) * HG_EXPAND)
                    a = scores_fn(hd, rows, cols).astype(BF16)
                    vc = v_s[rows, cols]
                    skv = skv_s[hd]
                    lhs = jnp.concatenate([qe_s[rows, cols], a], axis=1)
                    rhs = jnp.concatenate([skv.astype(BF16), vc], axis=0)
                    o_s[rows, cols] = _dot(lhs, rhs)
                    upd = lax.dot_general(kd_s[rows, cols], vc, _TN, preferred_element_type=F32)
                    ecol = jnp.broadcast_to(et[hd * HG_EXPAND:(hd + 1) * HG_EXPAND, c:c + 1],
                                            (HG_EXPAND, HG_EXPAND))
                    skv_s[hd] = skv * ecol + upd
            gate = p_s[:, 3 * d_tok:4 * d_tok]
            gm_s[...] = gate * jax.nn.sigmoid(gate) * onorm_ref[...]

        @pl.when(direct_ok)
        def _():
            kx_s = xw_s.at[0]
            kx_s[...] = (kk_s[...] * jnp.exp(-b_s[...])).astype(BF16)
            causal = masks_ref[len(LEVELS) + 1] != 0.0

            def scores(hd, rows, cols):
                s = lax.dot_general(qe_s[rows, cols], kx_s[rows, cols], _NT, preferred_element_type=F32)
                return jnp.where(causal, s, 0.0)

            mix(scores)

        @pl.when(jnp.logical_not(direct_ok))
        def _():
            for li, w in enumerate(LEVELS):
                for g0 in range(0, ts, 2 * w):
                    m = b_s[g0 + w:g0 + w + 1, :]
                    xk = kk_s[g0:g0 + w, :] * jnp.exp(m - b_s[g0:g0 + w, :])
                    xq = p_s[g0 + w:g0 + 2 * w, :d_tok] * jnp.exp(b_s[g0 + w:g0 + 2 * w, :] - m)
                    xw_s[li, g0:g0 + 2 * w, :] = jnp.concatenate([xk, xq], axis=0).astype(BF16)

            def pd_body(i, r_base):
                r0 = pl.multiple_of(i * 2 * DIAG, 2 * DIAG)
                g0 = r_base + r0
                q16 = p_s[pl.ds(g0, 2 * DIAG), :d_tok]
                b16 = b_s[pl.ds(g0, 2 * DIAG), :]
                for c in range(DIAG):
                    def partner(ref):
                        top = jnp.broadcast_to(ref[pl.ds(g0 + c, 1), :], (DIAG, d_tok))
                        bot = jnp.broadcast_to(ref[pl.ds(g0 + DIAG + c, 1), :], (DIAG, d_tok))
                        return jnp.concatenate([top, bot], axis=0)
                    val = q16 * partner(kk_s) * jnp.exp(jnp.minimum(b16 - partner(b_s), 0.0))
                    vb = val.astype(BF16)
                    for hd in range(n_heads):
                        col = (hd * DIAG + c) * HG_EXPAND
                        pd_s[pl.ds(r0, 2 * DIAG), col:col + HG_EXPAND] = vb[:, hd * HG_EXPAND:(hd + 1) * HG_EXPAND]
                return r_base

            kdiag = DIAG * HG_EXPAND
            pd_rows = pd_s.shape[0]
            for r_base in range(0, ts, pd_rows):
                lax.fori_loop(0, pd_rows // (2 * DIAG), pd_body, r_base)
                for hd in range(n_heads):
                    rd_s[hd, r_base:r_base + pd_rows, :] = _dot(pd_s[:, hd * kdiag:(hd + 1) * kdiag], esel_ref[...])

            def scores(hd, rows, cols):
                a = rd_s[hd, rows, :CHUNK] * masks_ref[len(LEVELS)]
                for li in range(len(LEVELS)):
                    xl = xw_s[li, rows, cols]
                    a = a + lax.dot_general(xl, xl, _NT, preferred_element_type=F32) * masks_ref[li]
                return a

            mix(scores)


def _gmlp_layer_kernel(x_ref, mem_ref, gmix_ref, gmem_ref, wkv_ref, win_ref, lng_ref, lnb_ref,
                       ws_ref, bias_ref, wout_ref, out_ref, kbd_s, vbd_s, p_s, vn_s, xprev_s, *, n_tiles):
    ts = x_ref.shape[0]
    d_tok = vn_s.shape[1]
    n_groups = ws_ref.shape[0]
    gdim = d_tok // n_groups
    pp = p_s.at[1]

    i = pl.program_id(0)

    @pl.when(i == 0)
    def _():
        pp[...] = jnp.zeros(pp.shape, F32)
        xprev_s[...] = jnp.zeros_like(xprev_s)
        kbd_s[...] = jnp.zeros_like(kbd_s)
        vbd_s[...] = jnp.zeros_like(vbd_s)

    x = x_ref[...]
    h = _rmsnorm(x, gmix_ref[...]).astype(BF16)
    p_s[0] = _dot(h, win_ref[...])

    pz = pp[:, :2 * d_tok]
    z = 0.5 * pz * (1.0 + lax.erf(pz * (2.0 ** -0.5)))
    u = z[:, :d_tok]
    v = z[:, d_tok:]
    mu = jnp.mean(v, axis=-1, keepdims=True)
    vc = v - mu
    var = jnp.mean(vc * vc, axis=-1, keepdims=True)
    vn_s[...] = (vc * lax.rsqrt(var + EPS) * lng_ref[...] + lnb_ref[...]).astype(BF16)

    tril = (lax.broadcasted_iota(jnp.int32, (GM_CHUNK, GM_CHUNK), 0)
            >= lax.broadcasted_iota(jnp.int32, (GM_CHUNK, GM_CHUNK), 1))
    heads = []
    for g in range(n_groups):
        cols = slice(g * gdim, (g + 1) * gdim)
        wg = jnp.where(tril, ws_ref[g], 0.0).astype(BF16)
        sv = [_dot(wg, vn_s[n * GM_CHUNK:(n + 1) * GM_CHUNK, cols]) + bias_ref[:, cols]
              for n in range(ts // GM_CHUNK)]
        heads.append((u[:, cols] * jnp.concatenate(sv, axis=0)).astype(BF16))
    heads.append(_mem_attention(pp[:, 2 * d_tok:], kbd_s, vbd_s).astype(BF16))
    out_ref[...] = xprev_s[...] + _dot(jnp.concatenate(heads, axis=1), wout_ref[...])
    xprev_s[...] = x
    pp[...] = p_s[0]

    @pl.when(jnp.logical_and(i % n_tiles == 0, i < pl.num_programs(0) - 1))
    def _():
        _mem_kv_init(mem_ref, gmem_ref, wkv_ref, kbd_s, vbd_s)


def _ffn_kernel(x_ref, g_ref, w1_ref, w2_ref, gfin_ref, out_ref, *, final_norm):
    d_ff = w2_ref.shape[0]
    x = x_ref[...]
    h = _rmsnorm(x, g_ref[...]).astype(BF16)
    y = x
    for c0 in range(0, d_ff, FFN_CHUNK):
        gate = _dot(h, w1_ref[:, c0:c0 + FFN_CHUNK])
        up = _dot(h, w1_ref[:, d_ff + c0:d_ff + c0 + FFN_CHUNK])
        act = (gate * jax.nn.sigmoid(gate) * up).astype(BF16)
        y = y + _dot(act, w2_ref[c0:c0 + FFN_CHUNK, :])
    if final_norm:
        y = _rmsnorm(y, gfin_ref[...])
    out_ref[...] = y


def _const_spec(shape):
    nd = len(shape)
    return pl.BlockSpec(shape, lambda *_: (0,) * nd)


def _layer_spec(stacked, layer):
    nd = stacked.ndim - 1
    return pl.BlockSpec((None,) + stacked.shape[1:], lambda *_: (layer,) + (0,) * nd,
                        pipeline_mode=pl.Buffered(1))


def _skewed_specs(D, M, ts, n_tiles, total):
    def tile(t):
        return (t // n_tiles, t % n_tiles, 0)
    x_spec = pl.BlockSpec((None, ts, D), lambda i: tile(jnp.minimum(i, total - 1)))
    out_spec = pl.BlockSpec((None, ts, D), lambda i: tile(jnp.maximum(i - 1, 0)))
    mem_spec = pl.BlockSpec((None, M, D), lambda i: (jnp.minimum(i, total - 1) // n_tiles, 0, 0))
    return x_spec, out_spec, mem_spec


def _hgrn_layer(x, mem, gmix, gmem, wkv, win, lb, onorm, wout, layer_idx, mixer_idx):
    B, S, D = x.shape
    M = mem.shape[1]
    d_tok = lb.shape[1]
    d_mem = D - d_tok
    n_heads = d_tok // HG_EXPAND
    ts = min(HG_TILE, S)
    n_tiles = S // ts
    assert S % ts == 0 and ts % CHUNK == 0 and d_tok % HG_EXPAND == 0 and ts // CHUNK <= SUBLANES

    kidx = jnp.arange(DIAG * HG_EXPAND) // HG_EXPAND
    esel = (kidx[:, None] == (jnp.arange(LANES)[None, :] % DIAG)).astype(BF16)
    t = jnp.arange(CHUNK)[:, None]
    s = jnp.arange(CHUNK)[None, :]
    masks = [(((t // w) % 2 == 1) & (s // w == t // w - 1)) for w in LEVELS]
    masks.append((s // DIAG == t // DIAG) & (s % DIAG <= t % DIAG))
    masks.append(s <= t)
    masks = jnp.stack(masks).astype(F32)

    total = B * n_tiles
    x_spec, out_spec, mem_spec = _skewed_specs(D, M, ts, n_tiles, total)
    kernel = functools.partial(_hgrn_layer_kernel, layer_idx=layer_idx, n_tiles=n_tiles)
    return pl.pallas_call(
        kernel,
        grid=(total + 1,),
        in_specs=[x_spec, mem_spec, _const_spec((1, D)), _const_spec((1, D)), _layer_spec(wkv, layer_idx),
                  _layer_spec(win, mixer_idx), _const_spec(lb.shape), _const_spec((1, d_tok)),
                  _layer_spec(wout, layer_idx), _const_spec(esel.shape),
                  _const_spec(masks.shape)],
        out_specs=out_spec,
        out_shape=jax.ShapeDtypeStruct(x.shape, F32),
        scratch_shapes=[
            pltpu.VMEM((n_heads, HG_EXPAND, HG_EXPAND), F32),
            pltpu.VMEM((d_mem, MEM_HEADS * M), BF16),
            pltpu.VMEM((MEM_HEADS * M, d_mem), BF16),
            pltpu.VMEM((ts, win.shape[2]), F32),
            pltpu.VMEM((ts, d_tok), F32),
            pltpu.VMEM((ts, d_tok), F32),
            pltpu.VMEM((ts, d_tok), BF16),
            pltpu.VMEM((ts, d_tok), BF16),
            pltpu.VMEM((ts, d_tok), BF16),
            pltpu.VMEM((len(LEVELS), ts, d_tok), BF16),
            pltpu.VMEM((min(ts, PD_ROWS), n_heads * DIAG * HG_EXPAND), BF16),
            pltpu.VMEM((n_heads, ts, LANES), F32),
            pltpu.VMEM((SUBLANES, d_tok), F32),
            pltpu.VMEM((ts, d_tok), F32),
            pltpu.VMEM((ts, d_tok), F32),
            pltpu.VMEM((ts, d_mem), BF16),
            pltpu.VMEM((ts, D), F32),
        ],
        compiler_params=pltpu.CompilerParams(
            dimension_semantics=("arbitrary",), vmem_limit_bytes=VMEM_LIMIT),
        name="hgrn_layer",
    )(x, mem, gmix.reshape(1, D), gmem.reshape(1, D), wkv, win, lb, onorm.reshape(1, d_tok), wout,
      esel, masks)


def _gmlp_layer(x, mem, gmix, gmem, wkv, win, lng, lnb, ws, bs, wout, layer_idx, mixer_idx):
    B, S, D = x.shape
    M = mem.shape[1]
    d_tok = lng.shape[0]
    d_mem = D - d_tok
    n_groups = ws.shape[0]
    gdim = d_tok // n_groups
    ts = min(GM_TILE, S)
    n_tiles = S // ts
    assert S % ts == 0 and ts % GM_CHUNK == 0 and ws.shape[1:] == (GM_CHUNK, GM_CHUNK)
    bias = jnp.repeat(bs.astype(F32).T, gdim, axis=1)

    total = B * n_tiles
    x_spec, out_spec, mem_spec = _skewed_specs(D, M, ts, n_tiles, total)
    return pl.pallas_call(
        functools.partial(_gmlp_layer_kernel, n_tiles=n_tiles),
        grid=(total + 1,),
        in_specs=[x_spec, mem_spec, _const_spec((1, D)), _const_spec((1, D)), _layer_spec(wkv, layer_idx),
                  _layer_spec(win, mixer_idx), _const_spec((1, d_tok)), _const_spec((1, d_tok)),
                  _const_spec(ws.shape), _const_spec(bias.shape), _layer_spec(wout, layer_idx)],
        out_specs=out_spec,
        out_shape=jax.ShapeDtypeStruct(x.shape, F32),
        scratch_shapes=[
            pltpu.VMEM((d_mem, MEM_HEADS * M), BF16),
            pltpu.VMEM((MEM_HEADS * M, d_mem), BF16),
            pltpu.VMEM((2, ts, win.shape[2]), F32),
            pltpu.VMEM((ts, d_tok), BF16),
            pltpu.VMEM((ts, D), F32),
        ],
        compiler_params=pltpu.CompilerParams(
            dimension_semantics=("arbitrary",), vmem_limit_bytes=VMEM_LIMIT),
        name="gmlp_layer",
    )(x, mem, gmix.reshape(1, D), gmem.reshape(1, D), wkv, win, lng.reshape(1, d_tok),
      lnb.reshape(1, d_tok), ws, bias, wout)


def _ffn_layer(x, g, w1, w2, gfin, layer_idx, final_norm):
    B, S, D = x.shape
    T = B * S
    tm = min(FFN_TILE, T)
    assert T % tm == 0 and w2.shape[1] % FFN_CHUNK == 0
    x2 = x.reshape(T, D)
    row_spec = pl.BlockSpec((tm, D), lambda i: (i, 0))
    kernel = functools.partial(_ffn_kernel, final_norm=final_norm)
    out = pl.pallas_call(
        kernel,
        grid=(T // tm,),
        in_specs=[row_spec, _const_spec((1, D)), _layer_spec(w1, layer_idx), _layer_spec(w2, layer_idx),
                  _const_spec((1, D))],
        out_specs=row_spec,
        out_shape=jax.ShapeDtypeStruct((T, D), F32),
        compiler_params=pltpu.CompilerParams(
            dimension_semantics=("arbitrary",), vmem_limit_bytes=VMEM_LIMIT),
        name="ffn_layer",
    )(x2, g.reshape(1, D), w1, w2, gfin.reshape(1, D))
    return out.reshape(B, S, D)


def kernel(x, mem, mix_norm, mem_norm, w_mem_kv, w_out, hg_w_in, hg_lb, hg_onorm, gm_w_in, gm_ln_g,
           gm_ln_b, gm_ws, gm_bs, ffn_norm, w_ffn_in, w_ffn_out, final_norm):
    depth = mix_norm.shape[0]
    n_mixers = 2
    x = x.astype(F32)
    mem = mem.astype(F32)
    wkv, wout, hg_win, gm_win, w1, w2 = (w.astype(BF16) for w in (w_mem_kv, w_out, hg_w_in, gm_w_in,
                                                                  w_ffn_in, w_ffn_out))
    for i in range(depth):
        j = i // n_mixers
        if i % n_mixers == 0:
            x = _hgrn_layer(x, mem, mix_norm[i], mem_norm[i], wkv, hg_win, hg_lb.astype(F32), hg_onorm[j],
                            wout, layer_idx=i, mixer_idx=j)
        else:
            x = _gmlp_layer(x, mem, mix_norm[i], mem_norm[i], wkv, gm_win, gm_ln_g[j], gm_ln_b[j],
                            gm_ws[j], gm_bs[j], wout, layer_idx=i, mixer_idx=j)
        x = _ffn_layer(x, ffn_norm[i], w1, w2, final_norm, layer_idx=i, final_norm=(i == depth - 1))
    return x
```

```python
import functools

import jax
import jax.numpy as jnp
from jax import lax
from jax.experimental import pallas as pl
from jax.experimental.pallas import tpu as pltpu

F32 = jnp.float32
BF16 = jnp.bfloat16
EPS = 1e-6

LANES = 128
SUBLANES = 8
HG_EXPAND = 128
CHUNK = 64
LEVELS = (32, 16, 8)
DIAG = 8
DIRECT_MIN_B = -80.0
GM_CHUNK = 128
MEM_HEADS = 4
PD_ROWS = 256
HG_TILE = 512
GM_TILE = 512
FFN_TILE = 1024
FFN_CHUNK = 256
VMEM_LIMIT = 56 * 1024 * 1024

_NT = (((1,), (1,)), ((), ()))
_TN = (((0,), (0,)), ((), ()))


def _dot(a, b):
    return jnp.dot(a, b, preferred_element_type=F32)


def _rmsnorm(x, g):
    return x * lax.rsqrt(jnp.mean(x * x, axis=-1, keepdims=True) + EPS) * g


def _mem_kv_init(mem_ref, gmem_ref, wkv_ref, kbd_s, vbd_s):
    d_mem = vbd_s.shape[1]
    hdim = d_mem // MEM_HEADS
    mlen = kbd_s.shape[1] // MEM_HEADS
    mn = _rmsnorm(mem_ref[...], gmem_ref[...])
    kv = _dot(mn.astype(BF16), wkv_ref[...])
    k = kv[:, :d_mem] * (hdim ** -0.5)
    v = kv[:, d_mem:]
    kt = k.T
    row_head = lax.broadcasted_iota(jnp.int32, kt.shape, 0) // hdim
    col_head = lax.broadcasted_iota(jnp.int32, v.shape, 1) // hdim
    for h in range(MEM_HEADS):
        kbd_s[:, h * mlen:(h + 1) * mlen] = jnp.where(row_head == h, kt, 0.0).astype(BF16)
        vbd_s[h * mlen:(h + 1) * mlen, :] = jnp.where(col_head == h, v, 0.0).astype(BF16)


def _mem_attention(qm, kbd_s, vbd_s):
    mlen = kbd_s.shape[1] // MEM_HEADS
    s_all = _dot(qm.astype(BF16), kbd_s[...])
    probs = []
    for h in range(MEM_HEADS):
        sh = s_all[:, h * mlen:(h + 1) * mlen]
        e = jnp.exp(sh - jnp.max(sh, axis=-1, keepdims=True))
        probs.append((e / jnp.sum(e, axis=-1, keepdims=True)).astype(BF16))
    return _dot(jnp.concatenate(probs, axis=1), vbd_s[...])


def _out_proj(parts, wout_ref):
    acc = None
    r0 = 0
    for part in parts:
        term = _dot(part, wout_ref[r0:r0 + part.shape[1], :])
        acc = term if acc is None else acc + term
        r0 += part.shape[1]
    return acc


def _chunk_cumsum(a):
    rows, d = a.shape
    row = lax.broadcasted_iota(jnp.int32, (SUBLANES, d), 0)
    outs = []
    carry = None
    for i in range(rows // SUBLANES):
        blk = a[i * SUBLANES:(i + 1) * SUBLANES, :]
        shift = 1
        while shift < SUBLANES:
            blk = blk + jnp.where(row >= shift, pltpu.roll(blk, shift, axis=0), 0.0)
            shift *= 2
        if i % (CHUNK // SUBLANES) != 0:
            blk = blk + carry
        carry = jnp.broadcast_to(blk[SUBLANES - 1:SUBLANES, :], (SUBLANES, d))
        outs.append(blk)
    return jnp.concatenate(outs, axis=0)


def _hgrn_layer_kernel(x_ref, mem_ref, gmix_ref, gmem_ref, wkv_ref, win_ref, lb_ref, onorm_ref,
                       wout_ref, esel_ref, masks_ref, out_ref,
                       skv_s, kbd_s, vbd_s, p_s, b_s, kk_s, v_s, qe_s, kd_s, xw_s, pd_s, rd_s,
                       ebl_s, o_s, gm_s, mo_s, xprev_s, *, layer_idx, n_tiles):
    ts = x_ref.shape[0]
    d_tok = b_s.shape[1]
    n_heads = d_tok // HG_EXPAND
    n_chunks = ts // CHUNK
    i = pl.program_id(0)
    has_tile = i < pl.num_programs(0) - 1

    @pl.when(jnp.logical_and(i % n_tiles == 0, has_tile))
    def _():
        skv_s[...] = jnp.zeros_like(skv_s)
        _mem_kv_init(mem_ref, gmem_ref, wkv_ref, kbd_s, vbd_s)

    @pl.when(i == 0)
    def _():
        o_s[...] = jnp.zeros_like(o_s)
        gm_s[...] = jnp.zeros_like(gm_s)
        mo_s[...] = jnp.zeros_like(mo_s)
        xprev_s[...] = jnp.zeros_like(xprev_s)
        ebl_s[...] = jnp.zeros_like(ebl_s)

    x = x_ref[...]
    h = _rmsnorm(x, gmix_ref[...]).astype(BF16)
    p_s[...] = _dot(h, win_ref[...])

    heads = []
    for hd in range(n_heads):
        cols = slice(hd * HG_EXPAND, (hd + 1) * HG_EXPAND)
        oh = o_s[:, cols]
        on = oh * lax.rsqrt(jnp.mean(oh * oh, axis=-1, keepdims=True) + EPS)
        heads.append((on * gm_s[:, cols]).astype(BF16))
    parts = [jnp.concatenate(heads[k:k + 2], axis=1) for k in range(0, n_heads, 2)] + [mo_s[...]]
    out_ref[...] = xprev_s[...] + _out_proj(parts, wout_ref)
    xprev_s[...] = x

    gate = p_s[:, 3 * d_tok:4 * d_tok]
    gm_s[...] = gate * jax.nn.sigmoid(gate) * onorm_ref[...]
    mo_s[...] = _mem_attention(p_s[:, 4 * d_tok:], kbd_s, vbd_s).astype(BF16)

    lbl = lb_ref[...]
    lbe = jnp.exp(lbl - jnp.max(lbl, axis=0, keepdims=True))
    lbv = jnp.sum(lbe[:layer_idx + 1], axis=0, keepdims=True) / jnp.sum(lbe, axis=0, keepdims=True)

    f = lbv + (1.0 - lbv) * jax.nn.sigmoid(p_s[:, d_tok:2 * d_tok])
    kk_s[...] = 1.0 - f
    b = _chunk_cumsum(jnp.log(f))
    b_s[...] = b
    qe_s[...] = (p_s[:, :d_tok] * jnp.exp(b)).astype(BF16)
    v_s[...] = p_s[:, 2 * d_tok:3 * d_tok].astype(BF16)

    for c in range(n_chunks):
        rows = slice(c * CHUNK, (c + 1) * CHUNK)
        bl = b_s[(c + 1) * CHUNK - 1:(c + 1) * CHUNK, :]
        kd_s[rows, :] = (kk_s[rows, :] * jnp.exp(bl - b_s[rows, :])).astype(BF16)
        ebl_s[c:c + 1, :] = jnp.exp(bl)

    direct_ok = jnp.min(b) >= DIRECT_MIN_B

    @pl.when(has_tile)
    def _():
        def mix(scores_fn):
            et = ebl_s[...].T
            for c in range(n_chunks):
                rows = slice(c * CHUNK, (c + 1) * CHUNK)
                for hd in range(n_heads):
                    cols = slice(hd * HG_EXPAND, (hd + 1) * HG_EXPAND)
                    a = scores_fn(hd, rows, cols).astype(BF16)
                    vc = v_s[rows, cols]
                    skv = skv_s[hd]
                    lhs = jnp.concatenate([qe_s[rows, cols], a], axis=1)
                    rhs = jnp.concatenate([skv.astype(BF16), vc], axis=0)
                    o_s[rows, cols] = _dot(lhs, rhs)
                    upd = lax.dot_general(kd_s[rows, cols], vc, _TN, preferred_element_type=F32)
                    ecol = jnp.broadcast_to(et[hd * HG_EXPAND:(hd + 1) * HG_EXPAND, c:c + 1],
                                            (HG_EXPAND, HG_EXPAND))
                    skv_s[hd] = skv * ecol + upd

        @pl.when(direct_ok)
        def _():
            kx_s = xw_s.at[0]
            kx_s[...] = (kk_s[...] * jnp.exp(-b_s[...])).astype(BF16)
            causal = masks_ref[len(LEVELS) + 1] != 0.0

            def scores(hd, rows, cols):
                s = lax.dot_general(qe_s[rows, cols], kx_s[rows, cols], _NT, preferred_element_type=F32)
                return jnp.where(causal, s, 0.0)

            mix(scores)

        @pl.when(jnp.logical_not(direct_ok))
        def _():
            for li, w in enumerate(LEVELS):
                for g0 in range(0, ts, 2 * w):
                    m = b_s[g0 + w:g0 + w + 1, :]
                    xk = kk_s[g0:g0 + w, :] * jnp.exp(m - b_s[g0:g0 + w, :])
                    xq = p_s[g0 + w:g0 + 2 * w, :d_tok] * jnp.exp(b_s[g0 + w:g0 + 2 * w, :] - m)
                    xw_s[li, g0:g0 + 2 * w, :] = jnp.concatenate([xk, xq], axis=0).astype(BF16)

            def pd_body(i, r_base):
                r0 = pl.multiple_of(i * 2 * DIAG, 2 * DIAG)
                g0 = r_base + r0
                q16 = p_s[pl.ds(g0, 2 * DIAG), :d_tok]
                b16 = b_s[pl.ds(g0, 2 * DIAG), :]
                for c in range(DIAG):
                    def partner(ref):
                        top = jnp.broadcast_to(ref[pl.ds(g0 + c, 1), :], (DIAG, d_tok))
                        bot = jnp.broadcast_to(ref[pl.ds(g0 + DIAG + c, 1), :], (DIAG, d_tok))
                        return jnp.concatenate([top, bot], axis=0)
                    val = q16 * partner(kk_s) * jnp.exp(jnp.minimum(b16 - partner(b_s), 0.0))
                    vb = val.astype(BF16)
                    for hd in range(n_heads):
                        col = (hd * DIAG + c) * HG_EXPAND
                        pd_s[pl.ds(r0, 2 * DIAG), col:col + HG_EXPAND] = vb[:, hd * HG_EXPAND:(hd + 1) * HG_EXPAND]
                return r_base

            kdiag = DIAG * HG_EXPAND
            pd_rows = pd_s.shape[0]
            for r_base in range(0, ts, pd_rows):
                lax.fori_loop(0, pd_rows // (2 * DIAG), pd_body, r_base)
                for hd in range(n_heads):
                    rd_s[hd, r_base:r_base + pd_rows, :] = _dot(pd_s[:, hd * kdiag:(hd + 1) * kdiag], esel_ref[...])

            def scores(hd, rows, cols):
                a = rd_s[hd, rows, :CHUNK] * masks_ref[len(LEVELS)]
                for li in range(len(LEVELS)):
                    xl = xw_s[li, rows, cols]
                    a = a + lax.dot_general(xl, xl, _NT, preferred_element_type=F32) * masks_ref[li]
                return a

            mix(scores)


def _gmlp_layer_kernel(x_ref, mem_ref, gmix_ref, gmem_ref, wkv_ref, win_ref, lng_ref, lnb_ref,
                       ws_ref, bias_ref, wout_ref, out_ref, kbd_s, vbd_s, p_s, vn_s, xprev_s, *, n_tiles):
    ts = x_ref.shape[0]
    d_tok = vn_s.shape[1]
    n_groups = ws_ref.shape[0]
    gdim = d_tok // n_groups
    pp = p_s.at[1]

    i = pl.program_id(0)

    @pl.when(i == 0)
    def _():
        pp[...] = jnp.zeros(pp.shape, F32)
        xprev_s[...] = jnp.zeros_like(xprev_s)
        kbd_s[...] = jnp.zeros_like(kbd_s)
        vbd_s[...] = jnp.zeros_like(vbd_s)

    x = x_ref[...]
    h = _rmsnorm(x, gmix_ref[...]).astype(BF16)
    p_s[0] = _dot(h, win_ref[...])

    pz = pp[:, :2 * d_tok]
    z = 0.5 * pz * (1.0 + lax.erf(pz * (2.0 ** -0.5)))
    u = z[:, :d_tok]
    v = z[:, d_tok:]
    mu = jnp.mean(v, axis=-1, keepdims=True)
    vc = v - mu
    var = jnp.mean(vc * vc, axis=-1, keepdims=True)
    vn_s[...] = (vc * lax.rsqrt(var + EPS) * lng_ref[...] + lnb_ref[...]).astype(BF16)

    tril = (lax.broadcasted_iota(jnp.int32, (GM_CHUNK, GM_CHUNK), 0)
            >= lax.broadcasted_iota(jnp.int32, (GM_CHUNK, GM_CHUNK), 1))
    heads = []
    for g in range(n_groups):
        cols = slice(g * gdim, (g + 1) * gdim)
        wg = jnp.where(tril, ws_ref[g], 0.0).astype(BF16)
        sv = [_dot(wg, vn_s[n * GM_CHUNK:(n + 1) * GM_CHUNK, cols]) + bias_ref[:, cols]
              for n in range(ts // GM_CHUNK)]
        heads.append((u[:, cols] * jnp.concatenate(sv, axis=0)).astype(BF16))
    heads.append(_mem_attention(pp[:, 2 * d_tok:], kbd_s, vbd_s).astype(BF16))
    out_ref[...] = xprev_s[...] + _dot(jnp.concatenate(heads, axis=1), wout_ref[...])
    xprev_s[...] = x
    pp[...] = p_s[0]

    @pl.when(jnp.logical_and(i % n_tiles == 0, i < pl.num_programs(0) - 1))
    def _():
        _mem_kv_init(mem_ref, gmem_ref, wkv_ref, kbd_s, vbd_s)


def _ffn_kernel(x_ref, g_ref, w1_ref, w2_ref, gfin_ref, out_ref, *, final_norm):
    d_ff = w2_ref.shape[0]
    x = x_ref[...]
    h = _rmsnorm(x, g_ref[...]).astype(BF16)
    y = x
    for c0 in range(0, d_ff, FFN_CHUNK):
        gate = _dot(h, w1_ref[:, c0:c0 + FFN_CHUNK])
        up = _dot(h, w1_ref[:, d_ff + c0:d_ff + c0 + FFN_CHUNK])
        act = (gate * jax.nn.sigmoid(gate) * up).astype(BF16)
        y = y + _dot(act, w2_ref[c0:c0 + FFN_CHUNK, :])
    if final_norm:
        y = _rmsnorm(y, gfin_ref[...])
    out_ref[...] = y


def _const_spec(shape):
    nd = len(shape)
    return pl.BlockSpec(shape, lambda *_: (0,) * nd)


def _layer_spec(stacked, layer):
    nd = stacked.ndim - 1
    return pl.BlockSpec((None,) + stacked.shape[1:], lambda *_: (layer,) + (0,) * nd,
                        pipeline_mode=pl.Buffered(1))


def _skewed_specs(D, M, ts, n_tiles, total):
    def tile(t):
        return (t // n_tiles, t % n_tiles, 0)
    x_spec = pl.BlockSpec((None, ts, D), lambda i: tile(jnp.minimum(i, total - 1)))
    out_spec = pl.BlockSpec((None, ts, D), lambda i: tile(jnp.maximum(i - 1, 0)))
    mem_spec = pl.BlockSpec((None, M, D), lambda i: (jnp.minimum(i, total - 1) // n_tiles, 0, 0))
    return x_spec, out_spec, mem_spec


def _hgrn_layer(x, mem, gmix, gmem, wkv, win, lb, onorm, wout, layer_idx, mixer_idx):
    B, S, D = x.shape
    M = mem.shape[1]
    d_tok = lb.shape[1]
    d_mem = D - d_tok
    n_heads = d_tok // HG_EXPAND
    ts = min(HG_TILE, S)
    n_tiles = S // ts
    assert S % ts == 0 and ts % CHUNK == 0 and d_tok % HG_EXPAND == 0 and ts // CHUNK <= SUBLANES

    kidx = jnp.arange(DIAG * HG_EXPAND) // HG_EXPAND
    esel = (kidx[:, None] == (jnp.arange(LANES)[None, :] % DIAG)).astype(BF16)
    t = jnp.arange(CHUNK)[:, None]
    s = jnp.arange(CHUNK)[None, :]
    masks = [(((t // w) % 2 == 1) & (s // w == t // w - 1)) for w in LEVELS]
    masks.append((s // DIAG == t // DIAG) & (s % DIAG <= t % DIAG))
    masks.append(s <= t)
    masks = jnp.stack(masks).astype(F32)

    total = B * n_tiles
    x_spec, out_spec, mem_spec = _skewed_specs(D, M, ts, n_tiles, total)
    kernel = functools.partial(_hgrn_layer_kernel, layer_idx=layer_idx, n_tiles=n_tiles)
    return pl.pallas_call(
        kernel,
        grid=(total + 1,),
        in_specs=[x_spec, mem_spec, _const_spec((1, D)), _const_spec((1, D)), _layer_spec(wkv, layer_idx),
                  _layer_spec(win, mixer_idx), _const_spec(lb.shape), _const_spec((1, d_tok)),
                  _layer_spec(wout, layer_idx), _const_spec(esel.shape),
                  _const_spec(masks.shape)],
        out_specs=out_spec,
        out_shape=jax.ShapeDtypeStruct(x.shape, F32),
        scratch_shapes=[
            pltpu.VMEM((n_heads, HG_EXPAND, HG_EXPAND), F32),
            pltpu.VMEM((d_mem, MEM_HEADS * M), BF16),
            pltpu.VMEM((MEM_HEADS * M, d_mem), BF16),
            pltpu.VMEM((ts, win.shape[2]), F32),
            pltpu.VMEM((ts, d_tok), F32),
            pltpu.VMEM((ts, d_tok), F32),
            pltpu.VMEM((ts, d_tok), BF16),
            pltpu.VMEM((ts, d_tok), BF16),
            pltpu.VMEM((ts, d_tok), BF16),
            pltpu.VMEM((len(LEVELS), ts, d_tok), BF16),
            pltpu.VMEM((min(ts, PD_ROWS), n_heads * DIAG * HG_EXPAND), BF16),
            pltpu.VMEM((n_heads, ts, LANES), F32),
            pltpu.VMEM((SUBLANES, d_tok), F32),
            pltpu.VMEM((ts, d_tok), F32),
            pltpu.VMEM((ts, d_tok), F32),
            pltpu.VMEM((ts, d_mem), BF16),
            pltpu.VMEM((ts, D), F32),
        ],
        compiler_params=pltpu.CompilerParams(
            dimension_semantics=("arbitrary",), vmem_limit_bytes=VMEM_LIMIT),
        name="hgrn_layer",
    )(x, mem, gmix.reshape(1, D), gmem.reshape(1, D), wkv, win, lb, onorm.reshape(1, d_tok), wout,
      esel, masks)


def _gmlp_layer(x, mem, gmix, gmem, wkv, win, lng, lnb, ws, bs, wout, layer_idx, mixer_idx):
    B, S, D = x.shape
    M = mem.shape[1]
    d_tok = lng.shape[0]
    d_mem = D - d_tok
    n_groups = ws.shape[0]
    gdim = d_tok // n_groups
    ts = min(GM_TILE, S)
    n_tiles = S // ts
    assert S % ts == 0 and ts % GM_CHUNK == 0 and ws.shape[1:] == (GM_CHUNK, GM_CHUNK)
    bias = jnp.repeat(bs.astype(F32).T, gdim, axis=1)

    total = B * n_tiles
    x_spec, out_spec, mem_spec = _skewed_specs(D, M, ts, n_tiles, total)
    return pl.pallas_call(
        functools.partial(_gmlp_layer_kernel, n_tiles=n_tiles),
        grid=(total + 1,),
        in_specs=[x_spec, mem_spec, _const_spec((1, D)), _const_spec((1, D)), _layer_spec(wkv, layer_idx),
                  _layer_spec(win, mixer_idx), _const_spec((1, d_tok)), _const_spec((1, d_tok)),
                  _const_spec(ws.shape), _const_spec(bias.shape), _layer_spec(wout, layer_idx)],
        out_specs=out_spec,
        out_shape=jax.ShapeDtypeStruct(x.shape, F32),
        scratch_shapes=[
            pltpu.VMEM((d_mem, MEM_HEADS * M), BF16),
            pltpu.VMEM((MEM_HEADS * M, d_mem), BF16),
            pltpu.VMEM((2, ts, win.shape[2]), F32),
            pltpu.VMEM((ts, d_tok), BF16),
            pltpu.VMEM((ts, D), F32),
        ],
        compiler_params=pltpu.CompilerParams(
            dimension_semantics=("arbitrary",), vmem_limit_bytes=VMEM_LIMIT),
        name="gmlp_layer",
    )(x, mem, gmix.reshape(1, D), gmem.reshape(1, D), wkv, win, lng.reshape(1, d_tok),
      lnb.reshape(1, d_tok), ws, bias, wout)


def _ffn_layer(x, g, w1, w2, gfin, layer_idx, final_norm):
    B, S, D = x.shape
    T = B * S
    tm = min(FFN_TILE, T)
    assert T % tm == 0 and w2.shape[1] % FFN_CHUNK == 0
    x2 = x.reshape(T, D)
    row_spec = pl.BlockSpec((tm, D), lambda i: (i, 0))
    kernel = functools.partial(_ffn_kernel, final_norm=final_norm)
    out = pl.pallas_call(
        kernel,
        grid=(T // tm,),
        in_specs=[row_spec, _const_spec((1, D)), _layer_spec(w1, layer_idx), _layer_spec(w2, layer_idx),
                  _const_spec((1, D))],
        out_specs=row_spec,
        out_shape=jax.ShapeDtypeStruct((T, D), F32),
        compiler_params=pltpu.CompilerParams(
            dimension_semantics=("arbitrary",), vmem_limit_bytes=VMEM_LIMIT),
        name="ffn_layer",
    )(x2, g.reshape(1, D), w1, w2, gfin.reshape(1, D))
    return out.reshape(B, S, D)


def kernel(x, mem, mix_norm, mem_norm, w_mem_kv, w_out, hg_w_in, hg_lb, hg_onorm, gm_w_in, gm_ln_g,
           gm_ln_b, gm_ws, gm_bs, ffn_norm, w_ffn_in, w_ffn_out, final_norm):
    depth = mix_norm.shape[0]
    n_mixers = 2
    x = x.astype(F32)
    mem = mem.astype(F32)
    wkv, wout, hg_win, gm_win, w1, w2 = (w.astype(BF16) for w in (w_mem_kv, w_out, hg_w_in, gm_w_in,
                                                                  w_ffn_in, w_ffn_out))
    for i in range(depth):
        j = i // n_mixers
        if i % n_mixers == 0:
            x = _hgrn_layer(x, mem, mix_norm[i], mem_norm[i], wkv, hg_win, hg_lb.astype(F32), hg_onorm[j],
                            wout, layer_idx=i, mixer_idx=j)
        else:
            x = _gmlp_layer(x, mem, mix_norm[i], mem_norm[i], wkv, gm_win, gm_ln_g[j], gm_ln_b[j],
                            gm_ws[j], gm_bs[j], wout, layer_idx=i, mixer_idx=j)
        x = _ffn_layer(x, ffn_norm[i], w1, w2, final_norm, layer_idx=i, final_norm=(i == depth - 1))
    return x
```

```python
import functools

import jax
import jax.numpy as jnp
from jax import lax
from jax.experimental import pallas as pl
from jax.experimental.pallas import tpu as pltpu

F32 = jnp.float32
BF16 = jnp.bfloat16
EPS = 1e-6

LANES = 128
SUBLANES = 8
HG_EXPAND = 128
CHUNK = 64
LEVELS = (32, 16, 8)
DIAG = 8
DIRECT_MIN_B = -80.0
GM_CHUNK = 128
MEM_HEADS = 4
PD_ROWS = 256
HG_TILE = 512
GM_TILE = 512
FFN_TILE = 1024
FFN_CHUNK = 256
VMEM_LIMIT = 56 * 1024 * 1024

_NT = (((1,), (1,)), ((), ()))
_TN = (((0,), (0,)), ((), ()))


def _dot(a, b):
    return jnp.dot(a, b, preferred_element_type=F32)


def _rmsnorm(x, g):
    return x * lax.rsqrt(jnp.mean(x * x, axis=-1, keepdims=True) + EPS) * g


def _mem_kv_init(mem_ref, gmem_ref, wkv_ref, kbd_s, vbd_s):
    d_mem = vbd_s.shape[1]
    hdim = d_mem // MEM_HEADS
    mlen = kbd_s.shape[1] // MEM_HEADS
    mn = _rmsnorm(mem_ref[...], gmem_ref[...])
    kv = _dot(mn.astype(BF16), wkv_ref[...])
    k = kv[:, :d_mem] * (hdim ** -0.5)
    v = kv[:, d_mem:]
    kt = k.T
    row_head = lax.broadcasted_iota(jnp.int32, kt.shape, 0) // hdim
    col_head = lax.broadcasted_iota(jnp.int32, v.shape, 1) // hdim
    for h in range(MEM_HEADS):
        kbd_s[:, h * mlen:(h + 1) * mlen] = jnp.where(row_head == h, kt, 0.0).astype(BF16)
        vbd_s[h * mlen:(h + 1) * mlen, :] = jnp.where(col_head == h, v, 0.0).astype(BF16)


def _mem_attention(qm, kbd_s, vbd_s):
    mlen = kbd_s.shape[1] // MEM_HEADS
    s_all = _dot(qm.astype(BF16), kbd_s[...])
    probs = []
    for h in range(MEM_HEADS):
        sh = s_all[:, h * mlen:(h + 1) * mlen]
        e = jnp.exp(sh - jnp.max(sh, axis=-1, keepdims=True))
        probs.append((e / jnp.sum(e, axis=-1, keepdims=True)).astype(BF16))
    return _dot(jnp.concatenate(probs, axis=1), vbd_s[...])


def _out_proj(parts, wout_ref):
    acc = None
    r0 = 0
    for part in parts:
        term = _dot(part, wout_ref[r0:r0 + part.shape[1], :])
        acc = term if acc is None else acc + term
        r0 += part.shape[1]
    return acc


def _chunk_cumsum(a):
    rows, d = a.shape
    row = lax.broadcasted_iota(jnp.int32, (SUBLANES, d), 0)
    outs = []
    carry = None
    for i in range(rows // SUBLANES):
        blk = a[i * SUBLANES:(i + 1) * SUBLANES, :]
        shift = 1
        while shift < SUBLANES:
            blk = blk + jnp.where(row >= shift, pltpu.roll(blk, shift, axis=0), 0.0)
            shift *= 2
        if i % (CHUNK // SUBLANES) != 0:
            blk = blk + carry
        carry = jnp.broadcast_to(blk[SUBLANES - 1:SUBLANES, :], (SUBLANES, d))
        outs.append(blk)
    return jnp.concatenate(outs, axis=0)


def _hgrn_layer_kernel(x_ref, mem_ref, gmix_ref, gmem_ref, wkv_ref, win_ref, lb_ref, onorm_ref,
                       wout_ref, esel_ref, masks_ref, out_ref,
                       skv_s, kbd_s, vbd_s, p_s, b_s, kk_s, v_s, qe_s, kd_s, xw_s, pd_s, rd_s,
                       ebl_s, o_s, gm_s, mo_s, xprev_s, *, layer_idx, n_tiles):
    ts = x_ref.shape[0]
    d_tok = b_s.shape[1]
    n_heads = d_tok // HG_EXPAND
    n_chunks = ts // CHUNK
    i = pl.program_id(0)
    has_tile = i < pl.num_programs(0) - 1

    @pl.when(jnp.logical_and(i % n_tiles == 0, has_tile))
    def _():
        skv_s[...] = jnp.zeros_like(skv_s)
        _mem_kv_init(mem_ref, gmem_ref, wkv_ref, kbd_s, vbd_s)

    @pl.when(i == 0)
    def _():
        o_s[...] = jnp.zeros_like(o_s)
        gm_s[...] = jnp.zeros_like(gm_s)
        mo_s[...] = jnp.zeros_like(mo_s)
        xprev_s[...] = jnp.zeros_like(xprev_s)
        ebl_s[...] = jnp.zeros_like(ebl_s)

    x = x_ref[...]
    h = _rmsnorm(x, gmix_ref[...]).astype(BF16)
    q_cols, f_cols, i_cols, g_cols = (slice(k * d_tok, (k + 1) * d_tok) for k in range(4))
    m_cols = slice(4 * d_tok, win_ref.shape[1])
    for cols in (f_cols, m_cols, q_cols, g_cols, i_cols):
        p_s[:, cols] = _dot(h, win_ref[:, cols])

    heads = []
    for hd in range(n_heads):
        cols = slice(hd * HG_EXPAND, (hd + 1) * HG_EXPAND)
        oh = o_s[:, cols]
        on = oh * lax.rsqrt(jnp.mean(oh * oh, axis=-1, keepdims=True) + EPS)
        heads.append((on * gm_s[:, cols]).astype(BF16))
    parts = [jnp.concatenate(heads[k:k + 2], axis=1) for k in range(0, n_heads, 2)] + [mo_s[...]]
    out_ref[...] = xprev_s[...] + _out_proj(parts, wout_ref)
    xprev_s[...] = x

    gate = p_s[:, 3 * d_tok:4 * d_tok]
    gm_s[...] = gate * jax.nn.sigmoid(gate) * onorm_ref[...]
    mo_s[...] = _mem_attention(p_s[:, 4 * d_tok:], kbd_s, vbd_s).astype(BF16)

    lbl = lb_ref[...]
    lbe = jnp.exp(lbl - jnp.max(lbl, axis=0, keepdims=True))
    lbv = jnp.sum(lbe[:layer_idx + 1], axis=0, keepdims=True) / jnp.sum(lbe, axis=0, keepdims=True)

    f = lbv + (1.0 - lbv) * jax.nn.sigmoid(p_s[:, d_tok:2 * d_tok])
    kk_s[...] = 1.0 - f
    b = _chunk_cumsum(jnp.log(f))
    b_s[...] = b
    qe_s[...] = (p_s[:, :d_tok] * jnp.exp(b)).astype(BF16)
    v_s[...] = p_s[:, 2 * d_tok:3 * d_tok].astype(BF16)

    for c in range(n_chunks):
        rows = slice(c * CHUNK, (c + 1) * CHUNK)
        bl = b_s[(c + 1) * CHUNK - 1:(c + 1) * CHUNK, :]
        kd_s[rows, :] = (kk_s[rows, :] * jnp.exp(bl - b_s[rows, :])).astype(BF16)
        ebl_s[c:c + 1, :] = jnp.exp(bl)

    direct_ok = jnp.min(b) >= DIRECT_MIN_B

    @pl.when(has_tile)
    def _():
        def mix(scores_fn):
            et = ebl_s[...].T
            for c in range(n_chunks):
                rows = slice(c * CHUNK, (c + 1) * CHUNK)
                for hd in range(n_heads):
                    cols = slice(hd * HG_EXPAND, (hd + 1) * HG_EXPAND)
                    a = scores_fn(hd, rows, cols).astype(BF16)
                    vc = v_s[rows, cols]
                    skv = skv_s[hd]
                    lhs = jnp.concatenate([qe_s[rows, cols], a], axis=1)
                    rhs = jnp.concatenate([skv.astype(BF16), vc], axis=0)
                    o_s[rows, cols] = _dot(lhs, rhs)
                    upd = lax.dot_general(kd_s[rows, cols], vc, _TN, preferred_element_type=F32)
                    ecol = jnp.broadcast_to(et[hd * HG_EXPAND:(hd + 1) * HG_EXPAND, c:c + 1],
                                            (HG_EXPAND, HG_EXPAND))
                    skv_s[hd] = skv * ecol + upd

        @pl.when(direct_ok)
        def _():
            kx_s = xw_s.at[0]
            kx_s[...] = (kk_s[...] * jnp.exp(-b_s[...])).astype(BF16)
            causal = masks_ref[len(LEVELS) + 1] != 0.0

            def scores(hd, rows, cols):
                s = lax.dot_general(qe_s[rows, cols], kx_s[rows, cols], _NT, preferred_element_type=F32)
                return jnp.where(causal, s, 0.0)

            mix(scores)

        @pl.when(jnp.logical_not(direct_ok))
        def _():
            for li, w in enumerate(LEVELS):
                for g0 in range(0, ts, 2 * w):
                    m = b_s[g0 + w:g0 + w + 1, :]
                    xk = kk_s[g0:g0 + w, :] * jnp.exp(m - b_s[g0:g0 + w, :])
                    xq = p_s[g0 + w:g0 + 2 * w, :d_tok] * jnp.exp(b_s[g0 + w:g0 + 2 * w, :] - m)
                    xw_s[li, g0:g0 + 2 * w, :] = jnp.concatenate([xk, xq], axis=0).astype(BF16)

            def pd_body(i, r_base):
                r0 = pl.multiple_of(i * 2 * DIAG, 2 * DIAG)
                g0 = r_base + r0
                q16 = p_s[pl.ds(g0, 2 * DIAG), :d_tok]
                b16 = b_s[pl.ds(g0, 2 * DIAG), :]
                for c in range(DIAG):
                    def partner(ref):
                        top = jnp.broadcast_to(ref[pl.ds(g0 + c, 1), :], (DIAG, d_tok))
                        bot = jnp.broadcast_to(ref[pl.ds(g0 + DIAG + c, 1), :], (DIAG, d_tok))
                        return jnp.concatenate([top, bot], axis=0)
                    val = q16 * partner(kk_s) * jnp.exp(jnp.minimum(b16 - partner(b_s), 0.0))
                    vb = val.astype(BF16)
                    for hd in range(n_heads):
                        col = (hd * DIAG + c) * HG_EXPAND
                        pd_s[pl.ds(r0, 2 * DIAG), col:col + HG_EXPAND] = vb[:, hd * HG_EXPAND:(hd + 1) * HG_EXPAND]
                return r_base

            kdiag = DIAG * HG_EXPAND
            pd_rows = pd_s.shape[0]
            for r_base in range(0, ts, pd_rows):
                lax.fori_loop(0, pd_rows // (2 * DIAG), pd_body, r_base)
                for hd in range(n_heads):
                    rd_s[hd, r_base:r_base + pd_rows, :] = _dot(pd_s[:, hd * kdiag:(hd + 1) * kdiag], esel_ref[...])

            def scores(hd, rows, cols):
                a = rd_s[hd, rows, :CHUNK] * masks_ref[len(LEVELS)]
                for li in range(len(LEVELS)):
                    xl = xw_s[li, rows, cols]
                    a = a + lax.dot_general(xl, xl, _NT, preferred_element_type=F32) * masks_ref[li]
                return a

            mix(scores)


def _gmlp_layer_kernel(x_ref, mem_ref, gmix_ref, gmem_ref, wkv_ref, win_ref, lng_ref, lnb_ref,
                       ws_ref, bias_ref, wout_ref, out_ref, kbd_s, vbd_s, p_s, vn_s, xprev_s, *, n_tiles):
    ts = x_ref.shape[0]
    d_tok = vn_s.shape[1]
    n_groups = ws_ref.shape[0]
    gdim = d_tok // n_groups
    pp = p_s.at[1]

    i = pl.program_id(0)

    @pl.when(i == 0)
    def _():
        pp[...] = jnp.zeros(pp.shape, F32)
        xprev_s[...] = jnp.zeros_like(xprev_s)
        kbd_s[...] = jnp.zeros_like(kbd_s)
        vbd_s[...] = jnp.zeros_like(vbd_s)

    x = x_ref[...]
    h = _rmsnorm(x, gmix_ref[...]).astype(BF16)
    p_s[0] = _dot(h, win_ref[...])

    pz = pp[:, :2 * d_tok]
    z = 0.5 * pz * (1.0 + lax.erf(pz * (2.0 ** -0.5)))
    u = z[:, :d_tok]
    v = z[:, d_tok:]
    mu = jnp.mean(v, axis=-1, keepdims=True)
    vc = v - mu
    var = jnp.mean(vc * vc, axis=-1, keepdims=True)
    vn_s[...] = (vc * lax.rsqrt(var + EPS) * lng_ref[...] + lnb_ref[...]).astype(BF16)

    tril = (lax.broadcasted_iota(jnp.int32, (GM_CHUNK, GM_CHUNK), 0)
            >= lax.broadcasted_iota(jnp.int32, (GM_CHUNK, GM_CHUNK), 1))
    heads = []
    for g in range(n_groups):
        cols = slice(g * gdim, (g + 1) * gdim)
        wg = jnp.where(tril, ws_ref[g], 0.0).astype(BF16)
        sv = [_dot(wg, vn_s[n * GM_CHUNK:(n + 1) * GM_CHUNK, cols]) + bias_ref[:, cols]
              for n in range(ts // GM_CHUNK)]
        heads.append((u[:, cols] * jnp.concatenate(sv, axis=0)).astype(BF16))
    heads.append(_mem_attention(pp[:, 2 * d_tok:], kbd_s, vbd_s).astype(BF16))
    out_ref[...] = xprev_s[...] + _dot(jnp.concatenate(heads, axis=1), wout_ref[...])
    xprev_s[...] = x
    pp[...] = p_s[0]

    @pl.when(jnp.logical_and(i % n_tiles == 0, i < pl.num_programs(0) - 1))
    def _():
        _mem_kv_init(mem_ref, gmem_ref, wkv_ref, kbd_s, vbd_s)


def _ffn_kernel(x_ref, g_ref, w1_ref, w2_ref, gfin_ref, out_ref, *, final_norm):
    d_ff = w2_ref.shape[0]
    x = x_ref[...]
    h = _rmsnorm(x, g_ref[...]).astype(BF16)
    y = x
    for c0 in range(0, d_ff, FFN_CHUNK):
        gate = _dot(h, w1_ref[:, c0:c0 + FFN_CHUNK])
        up = _dot(h, w1_ref[:, d_ff + c0:d_ff + c0 + FFN_CHUNK])
        act = (gate * jax.nn.sigmoid(gate) * up).astype(BF16)
        y = y + _dot(act, w2_ref[c0:c0 + FFN_CHUNK, :])
    if final_norm:
        y = _rmsnorm(y, gfin_ref[...])
    out_ref[...] = y


def _const_spec(shape):
    nd = len(shape)
    return pl.BlockSpec(shape, lambda *_: (0,) * nd)


def _layer_spec(stacked, layer):
    nd = stacked.ndim - 1
    return pl.BlockSpec((None,) + stacked.shape[1:], lambda *_: (layer,) + (0,) * nd,
                        pipeline_mode=pl.Buffered(1))


def _skewed_specs(D, M, ts, n_tiles, total):
    def tile(t):
        return (t // n_tiles, t % n_tiles, 0)
    x_spec = pl.BlockSpec((None, ts, D), lambda i: tile(jnp.minimum(i, total - 1)))
    out_spec = pl.BlockSpec((None, ts, D), lambda i: tile(jnp.maximum(i - 1, 0)))
    mem_spec = pl.BlockSpec((None, M, D), lambda i: (jnp.minimum(i, total - 1) // n_tiles, 0, 0))
    return x_spec, out_spec, mem_spec


def _hgrn_layer(x, mem, gmix, gmem, wkv, win, lb, onorm, wout, layer_idx, mixer_idx):
    B, S, D = x.shape
    M = mem.shape[1]
    d_tok = lb.shape[1]
    d_mem = D - d_tok
    n_heads = d_tok // HG_EXPAND
    ts = min(HG_TILE, S)
    n_tiles = S // ts
    assert S % ts == 0 and ts % CHUNK == 0 and d_tok % HG_EXPAND == 0 and ts // CHUNK <= SUBLANES

    kidx = jnp.arange(DIAG * HG_EXPAND) // HG_EXPAND
    esel = (kidx[:, None] == (jnp.arange(LANES)[None, :] % DIAG)).astype(BF16)
    t = jnp.arange(CHUNK)[:, None]
    s = jnp.arange(CHUNK)[None, :]
    masks = [(((t // w) % 2 == 1) & (s // w == t // w - 1)) for w in LEVELS]
    masks.append((s // DIAG == t // DIAG) & (s % DIAG <= t % DIAG))
    masks.append(s <= t)
    masks = jnp.stack(masks).astype(F32)

    total = B * n_tiles
    x_spec, out_spec, mem_spec = _skewed_specs(D, M, ts, n_tiles, total)
    kernel = functools.partial(_hgrn_layer_kernel, layer_idx=layer_idx, n_tiles=n_tiles)
    return pl.pallas_call(
        kernel,
        grid=(total + 1,),
        in_specs=[x_spec, mem_spec, _const_spec((1, D)), _const_spec((1, D)), _layer_spec(wkv, layer_idx),
                  _layer_spec(win, mixer_idx), _const_spec(lb.shape), _const_spec((1, d_tok)),
                  _layer_spec(wout, layer_idx), _const_spec(esel.shape),
                  _const_spec(masks.shape)],
        out_specs=out_spec,
        out_shape=jax.ShapeDtypeStruct(x.shape, F32),
        scratch_shapes=[
            pltpu.VMEM((n_heads, HG_EXPAND, HG_EXPAND), F32),
            pltpu.VMEM((d_mem, MEM_HEADS * M), BF16),
            pltpu.VMEM((MEM_HEADS * M, d_mem), BF16),
            pltpu.VMEM((ts, win.shape[2]), F32),
            pltpu.VMEM((ts, d_tok), F32),
            pltpu.VMEM((ts, d_tok), F32),
            pltpu.VMEM((ts, d_tok), BF16),
            pltpu.VMEM((ts, d_tok), BF16),
            pltpu.VMEM((ts, d_tok), BF16),
            pltpu.VMEM((len(LEVELS), ts, d_tok), BF16),
            pltpu.VMEM((min(ts, PD_ROWS), n_heads * DIAG * HG_EXPAND), BF16),
            pltpu.VMEM((n_heads, ts, LANES), F32),
            pltpu.VMEM((SUBLANES, d_tok), F32),
            pltpu.VMEM((ts, d_tok), F32),
            pltpu.VMEM((ts, d_tok), F32),
            pltpu.VMEM((ts, d_mem), BF16),
            pltpu.VMEM((ts, D), F32),
        ],
        compiler_params=pltpu.CompilerParams(
            dimension_semantics=("arbitrary",), vmem_limit_bytes=VMEM_LIMIT),
        name="hgrn_layer",
    )(x, mem, gmix.reshape(1, D), gmem.reshape(1, D), wkv, win, lb, onorm.reshape(1, d_tok), wout,
      esel, masks)


def _gmlp_layer(x, mem, gmix, gmem, wkv, win, lng, lnb, ws, bs, wout, layer_idx, mixer_idx):
    B, S, D = x.shape
    M = mem.shape[1]
    d_tok = lng.shape[0]
    d_mem = D - d_tok
    n_groups = ws.shape[0]
    gdim = d_tok // n_groups
    ts = min(GM_TILE, S)
    n_tiles = S // ts
    assert S % ts == 0 and ts % GM_CHUNK == 0 and ws.shape[1:] == (GM_CHUNK, GM_CHUNK)
    bias = jnp.repeat(bs.astype(F32).T, gdim, axis=1)

    total = B * n_tiles
    x_spec, out_spec, mem_spec = _skewed_specs(D, M, ts, n_tiles, total)
    return pl.pallas_call(
        functools.partial(_gmlp_layer_kernel, n_tiles=n_tiles),
        grid=(total + 1,),
        in_specs=[x_spec, mem_spec, _const_spec((1, D)), _const_spec((1, D)), _layer_spec(wkv, layer_idx),
                  _layer_spec(win, mixer_idx), _const_spec((1, d_tok)), _const_spec((1, d_tok)),
                  _const_spec(ws.shape), _const_spec(bias.shape), _layer_spec(wout, layer_idx)],
        out_specs=out_spec,
        out_shape=jax.ShapeDtypeStruct(x.shape, F32),
        scratch_shapes=[
            pltpu.VMEM((d_mem, MEM_HEADS * M), BF16),
            pltpu.VMEM((MEM_HEADS * M, d_mem), BF16),
            pltpu.VMEM((2, ts, win.shape[2]), F32),
            pltpu.VMEM((ts, d_tok), BF16),
            pltpu.VMEM((ts, D), F32),
        ],
        compiler_params=pltpu.CompilerParams(
            dimension_semantics=("arbitrary",), vmem_limit_bytes=VMEM_LIMIT),
        name="gmlp_layer",
    )(x, mem, gmix.reshape(1, D), gmem.reshape(1, D), wkv, win, lng.reshape(1, d_tok),
      lnb.reshape(1, d_tok), ws, bias, wout)


def _ffn_layer(x, g, w1, w2, gfin, layer_idx, final_norm):
    B, S, D = x.shape
    T = B * S
    tm = min(FFN_TILE, T)
    assert T % tm == 0 and w2.shape[1] % FFN_CHUNK == 0
    x2 = x.reshape(T, D)
    row_spec = pl.BlockSpec((tm, D), lambda i: (i, 0))
    kernel = functools.partial(_ffn_kernel, final_norm=final_norm)
    out = pl.pallas_call(
        kernel,
        grid=(T // tm,),
        in_specs=[row_spec, _const_spec((1, D)), _layer_spec(w1, layer_idx), _layer_spec(w2, layer_idx),
                  _const_spec((1, D))],
        out_specs=row_spec,
        out_shape=jax.ShapeDtypeStruct((T, D), F32),
        compiler_params=pltpu.CompilerParams(
            dimension_semantics=("arbitrary",), vmem_limit_bytes=VMEM_LIMIT),
        name="ffn_layer",
    )(x2, g.reshape(1, D), w1, w2, gfin.reshape(1, D))
    return out.reshape(B, S, D)


def kernel(x, mem, mix_norm, mem_norm, w_mem_kv, w_out, hg_w_in, hg_lb, hg_onorm, gm_w_in, gm_ln_g,
           gm_ln_b, gm_ws, gm_bs, ffn_norm, w_ffn_in, w_ffn_out, final_norm):
    depth = mix_norm.shape[0]
    n_mixers = 2
    x = x.astype(F32)
    mem = mem.astype(F32)
    wkv, wout, hg_win, gm_win, w1, w2 = (w.astype(BF16) for w in (w_mem_kv, w_out, hg_w_in, gm_w_in,
                                                                  w_ffn_in, w_ffn_out))
    for i in range(depth):
        j = i // n_mixers
        if i % n_mixers == 0:
            x = _hgrn_layer(x, mem, mix_norm[i], mem_norm[i], wkv, hg_win, hg_lb.astype(F32), hg_onorm[j],
                            wout, layer_idx=i, mixer_idx=j)
        else:
            x = _gmlp_layer(x, mem, mix_norm[i], mem_norm[i], wkv, gm_win, gm_ln_g[j], gm_ln_b[j],
                            gm_ws[j], gm_bs[j], wout, layer_idx=i, mixer_idx=j)
        x = _ffn_layer(x, ffn_norm[i], w1, w2, final_norm, layer_idx=i, final_norm=(i == depth - 1))
    return x
```

```python
import functools

import jax
import jax.numpy as jnp
from jax import lax
from jax.experimental import pallas as pl
from jax.experimental.pallas import tpu as pltpu

F32 = jnp.float32
BF16 = jnp.bfloat16
EPS = 1e-6

LANES = 128
SUBLANES = 8
HG_EXPAND = 128
CHUNK = 64
LEVELS = (32, 16, 8)
DIAG = 8
DIRECT_MIN_B = -80.0
GM_CHUNK = 128
MEM_HEADS = 4
GM_ATT_GROUPS = 2
PD_ROWS = 256
HG_TILE = 512
GM_TILE = 512
FFN_TILE = 1024
FFN_CHUNK = 256
VMEM_LIMIT = 56 * 1024 * 1024

_NT = (((1,), (1,)), ((), ()))
_TN = (((0,), (0,)), ((), ()))


def _dot(a, b):
    return jnp.dot(a, b, preferred_element_type=F32)


def _rmsnorm(x, g):
    return x * lax.rsqrt(jnp.mean(x * x, axis=-1, keepdims=True) + EPS) * g


def _mem_kv_init(mem_ref, gmem_ref, wkv_ref, kbd_s, vbd_s):
    n_groups, gw, glen = kbd_s.shape
    d_mem = n_groups * gw
    hdim = d_mem // MEM_HEADS
    hp = gw // hdim
    mlen = glen // hp
    mn = _rmsnorm(mem_ref[...], gmem_ref[...])
    kv = _dot(mn.astype(BF16), wkv_ref[...])
    k = kv[:, :d_mem] * (hdim ** -0.5)
    v = kv[:, d_mem:]
    kt = k.T
    row_head = lax.broadcasted_iota(jnp.int32, (gw, mlen), 0) // hdim
    col_head = lax.broadcasted_iota(jnp.int32, (mlen, gw), 1) // hdim
    for g in range(n_groups):
        ktg = kt[g * gw:(g + 1) * gw, :]
        vg = v[:, g * gw:(g + 1) * gw]
        for h in range(hp):
            kbd_s[g, :, h * mlen:(h + 1) * mlen] = jnp.where(row_head == h, ktg, 0.0).astype(BF16)
            vbd_s[g, h * mlen:(h + 1) * mlen, :] = jnp.where(col_head == h, vg, 0.0).astype(BF16)


def _mem_attention(qm, kbd_s, vbd_s):
    n_groups, gw, glen = kbd_s.shape
    hp = MEM_HEADS // n_groups
    mlen = glen // hp
    outs = []
    for g in range(n_groups):
        s = _dot(qm[:, g * gw:(g + 1) * gw].astype(BF16), kbd_s[g])
        probs = []
        for h in range(hp):
            sh = s[:, h * mlen:(h + 1) * mlen]
            e = jnp.exp(sh - jnp.max(sh, axis=-1, keepdims=True))
            probs.append((e / jnp.sum(e, axis=-1, keepdims=True)).astype(BF16))
        outs.append(_dot(jnp.concatenate(probs, axis=1), vbd_s[g]))
    return outs[0] if n_groups == 1 else jnp.concatenate(outs, axis=1)


def _out_proj(parts, wout_ref):
    acc = None
    r0 = 0
    for part in parts:
        term = _dot(part, wout_ref[r0:r0 + part.shape[1], :])
        acc = term if acc is None else acc + term
        r0 += part.shape[1]
    return acc


def _chunk_cumsum(a):
    rows, d = a.shape
    row = lax.broadcasted_iota(jnp.int32, (SUBLANES, d), 0)
    outs = []
    carry = None
    for i in range(rows // SUBLANES):
        blk = a[i * SUBLANES:(i + 1) * SUBLANES, :]
        shift = 1
        while shift < SUBLANES:
            blk = blk + jnp.where(row >= shift, pltpu.roll(blk, shift, axis=0), 0.0)
            shift *= 2
        if i % (CHUNK // SUBLANES) != 0:
            blk = blk + carry
        carry = jnp.broadcast_to(blk[SUBLANES - 1:SUBLANES, :], (SUBLANES, d))
        outs.append(blk)
    return jnp.concatenate(outs, axis=0)


def _hgrn_layer_kernel(x_ref, mem_ref, gmix_ref, gmem_ref, wkv_ref, win_ref, lb_ref, onorm_ref,
                       wout_ref, esel_ref, masks_ref, out_ref,
                       skv_s, kbd_s, vbd_s, p_s, b_s, kk_s, v_s, qe_s, kd_s, xw_s, pd_s, rd_s,
                       ebl_s, o_s, gm_s, mo_s, xprev_s, *, layer_idx, n_tiles):
    ts = x_ref.shape[0]
    d_tok = b_s.shape[1]
    n_heads = d_tok // HG_EXPAND
    n_chunks = ts // CHUNK
    i = pl.program_id(0)
    has_tile = i < pl.num_programs(0) - 1

    @pl.when(jnp.logical_and(i % n_tiles == 0, has_tile))
    def _():
        skv_s[...] = jnp.zeros_like(skv_s)
        _mem_kv_init(mem_ref, gmem_ref, wkv_ref, kbd_s, vbd_s)

    @pl.when(i == 0)
    def _():
        o_s[...] = jnp.zeros_like(o_s)
        gm_s[...] = jnp.zeros_like(gm_s)
        mo_s[...] = jnp.zeros_like(mo_s)
        xprev_s[...] = jnp.zeros_like(xprev_s)
        ebl_s[...] = jnp.zeros_like(ebl_s)

    x = x_ref[...]
    h = _rmsnorm(x, gmix_ref[...]).astype(BF16)
    q_cols, f_cols, i_cols, g_cols = (slice(k * d_tok, (k + 1) * d_tok) for k in range(4))
    m_cols = slice(4 * d_tok, win_ref.shape[1])
    for cols in (f_cols, m_cols, q_cols, g_cols, i_cols):
        p_s[:, cols] = _dot(h, win_ref[:, cols])

    heads = []
    for hd in range(n_heads):
        cols = slice(hd * HG_EXPAND, (hd + 1) * HG_EXPAND)
        oh = o_s[:, cols]
        on = oh * lax.rsqrt(jnp.mean(oh * oh, axis=-1, keepdims=True) + EPS)
        heads.append((on * gm_s[:, cols]).astype(BF16))
    parts = [jnp.concatenate(heads[k:k + 2], axis=1) for k in range(0, n_heads, 2)] + [mo_s[...]]
    out_ref[...] = xprev_s[...] + _out_proj(parts, wout_ref)
    xprev_s[...] = x

    gate = p_s[:, 3 * d_tok:4 * d_tok]
    gm_s[...] = gate * jax.nn.sigmoid(gate) * onorm_ref[...]
    mo_s[...] = _mem_attention(p_s[:, 4 * d_tok:], kbd_s, vbd_s).astype(BF16)

    lbl = lb_ref[...]
    lbe = jnp.exp(lbl - jnp.max(lbl, axis=0, keepdims=True))
    lbv = jnp.sum(lbe[:layer_idx + 1], axis=0, keepdims=True) / jnp.sum(lbe, axis=0, keepdims=True)

    f = lbv + (1.0 - lbv) * jax.nn.sigmoid(p_s[:, d_tok:2 * d_tok])
    kk_s[...] = 1.0 - f
    b = _chunk_cumsum(jnp.log(f))
    b_s[...] = b
    qe_s[...] = (p_s[:, :d_tok] * jnp.exp(b)).astype(BF16)
    v_s[...] = p_s[:, 2 * d_tok:3 * d_tok].astype(BF16)

    for c in range(n_chunks):
        rows = slice(c * CHUNK, (c + 1) * CHUNK)
        bl = b_s[(c + 1) * CHUNK - 1:(c + 1) * CHUNK, :]
        kd_s[rows, :] = (kk_s[rows, :] * jnp.exp(bl - b_s[rows, :])).astype(BF16)
        ebl_s[c:c + 1, :] = jnp.exp(bl)

    direct_ok = jnp.min(b) >= DIRECT_MIN_B

    @pl.when(has_tile)
    def _():
        def mix(scores_fn):
            et = ebl_s[...].T
            for c in range(n_chunks):
                rows = slice(c * CHUNK, (c + 1) * CHUNK)
                for hd in range(n_heads):
                    cols = slice(hd * HG_EXPAND, (hd + 1) * HG_EXPAND)
                    a = scores_fn(hd, rows, cols).astype(BF16)
                    vc = v_s[rows, cols]
                    skv = skv_s[hd]
                    lhs = jnp.concatenate([qe_s[rows, cols], a], axis=1)
                    rhs = jnp.concatenate([skv.astype(BF16), vc], axis=0)
                    o_s[rows, cols] = _dot(lhs, rhs)
                    upd = lax.dot_general(kd_s[rows, cols], vc, _TN, preferred_element_type=F32)
                    ecol = jnp.broadcast_to(et[hd * HG_EXPAND:(hd + 1) * HG_EXPAND, c:c + 1],
                                            (HG_EXPAND, HG_EXPAND))
                    skv_s[hd] = skv * ecol + upd

        @pl.when(direct_ok)
        def _():
            kx_s = xw_s.at[0]
            kx_s[...] = (kk_s[...] * jnp.exp(-b_s[...])).astype(BF16)
            causal = masks_ref[len(LEVELS) + 1] != 0.0

            def scores(hd, rows, cols):
                s = lax.dot_general(qe_s[rows, cols], kx_s[rows, cols], _NT, preferred_element_type=F32)
                return jnp.where(causal, s, 0.0)

            mix(scores)

        @pl.when(jnp.logical_not(direct_ok))
        def _():
            for li, w in enumerate(LEVELS):
                for g0 in range(0, ts, 2 * w):
                    m = b_s[g0 + w:g0 + w + 1, :]
                    xk = kk_s[g0:g0 + w, :] * jnp.exp(m - b_s[g0:g0 + w, :])
                    xq = p_s[g0 + w:g0 + 2 * w, :d_tok] * jnp.exp(b_s[g0 + w:g0 + 2 * w, :] - m)
                    xw_s[li, g0:g0 + 2 * w, :] = jnp.concatenate([xk, xq], axis=0).astype(BF16)

            def pd_body(i, r_base):
                r0 = pl.multiple_of(i * 2 * DIAG, 2 * DIAG)
                g0 = r_base + r0
                q16 = p_s[pl.ds(g0, 2 * DIAG), :d_tok]
                b16 = b_s[pl.ds(g0, 2 * DIAG), :]
                for c in range(DIAG):
                    def partner(ref):
                        top = jnp.broadcast_to(ref[pl.ds(g0 + c, 1), :], (DIAG, d_tok))
                        bot = jnp.broadcast_to(ref[pl.ds(g0 + DIAG + c, 1), :], (DIAG, d_tok))
                        return jnp.concatenate([top, bot], axis=0)
                    val = q16 * partner(kk_s) * jnp.exp(jnp.minimum(b16 - partner(b_s), 0.0))
                    vb = val.astype(BF16)
                    for hd in range(n_heads):
                        col = (hd * DIAG + c) * HG_EXPAND
                        pd_s[pl.ds(r0, 2 * DIAG), col:col + HG_EXPAND] = vb[:, hd * HG_EXPAND:(hd + 1) * HG_EXPAND]
                return r_base

            kdiag = DIAG * HG_EXPAND
            pd_rows = pd_s.shape[0]
            for r_base in range(0, ts, pd_rows):
                lax.fori_loop(0, pd_rows // (2 * DIAG), pd_body, r_base)
                for hd in range(n_heads):
                    rd_s[hd, r_base:r_base + pd_rows, :] = _dot(pd_s[:, hd * kdiag:(hd + 1) * kdiag], esel_ref[...])

            def scores(hd, rows, cols):
                a = rd_s[hd, rows, :CHUNK] * masks_ref[len(LEVELS)]
                for li in range(len(LEVELS)):
                    xl = xw_s[li, rows, cols]
                    a = a + lax.dot_general(xl, xl, _NT, preferred_element_type=F32) * masks_ref[li]
                return a

            mix(scores)


def _gmlp_layer_kernel(x_ref, mem_ref, gmix_ref, gmem_ref, wkv_ref, win_ref, lng_ref, lnb_ref,
                       ws_ref, bias_ref, wout_ref, out_ref, kbd_s, vbd_s, p_s, vn_s, xprev_s, *, n_tiles):
    ts = x_ref.shape[0]
    d_tok = vn_s.shape[1]
    n_groups = ws_ref.shape[0]
    gdim = d_tok // n_groups
    pp = p_s.at[1]

    i = pl.program_id(0)

    @pl.when(i == 0)
    def _():
        pp[...] = jnp.zeros(pp.shape, F32)
        xprev_s[...] = jnp.zeros_like(xprev_s)
        kbd_s[...] = jnp.zeros_like(kbd_s)
        vbd_s[...] = jnp.zeros_like(vbd_s)

    x = x_ref[...]
    h = _rmsnorm(x, gmix_ref[...]).astype(BF16)
    p_s[0] = _dot(h, win_ref[...])

    pz = pp[:, :2 * d_tok]
    z = 0.5 * pz * (1.0 + lax.erf(pz * (2.0 ** -0.5)))
    u = z[:, :d_tok]
    v = z[:, d_tok:]
    mu = jnp.mean(v, axis=-1, keepdims=True)
    vc = v - mu
    var = jnp.mean(vc * vc, axis=-1, keepdims=True)
    vn_s[...] = (vc * lax.rsqrt(var + EPS) * lng_ref[...] + lnb_ref[...]).astype(BF16)

    tril = (lax.broadcasted_iota(jnp.int32, (GM_CHUNK, GM_CHUNK), 0)
            >= lax.broadcasted_iota(jnp.int32, (GM_CHUNK, GM_CHUNK), 1))
    heads = []
    for g in range(n_groups):
        cols = slice(g * gdim, (g + 1) * gdim)
        wg = jnp.where(tril, ws_ref[g], 0.0).astype(BF16)
        sv = [_dot(wg, vn_s[n * GM_CHUNK:(n + 1) * GM_CHUNK, cols]) + bias_ref[:, cols]
              for n in range(ts // GM_CHUNK)]
        heads.append((u[:, cols] * jnp.concatenate(sv, axis=0)).astype(BF16))
    heads.append(_mem_attention(pp[:, 2 * d_tok:], kbd_s, vbd_s).astype(BF16))
    out_ref[...] = xprev_s[...] + _dot(jnp.concatenate(heads, axis=1), wout_ref[...])
    xprev_s[...] = x
    pp[...] = p_s[0]

    @pl.when(jnp.logical_and(i % n_tiles == 0, i < pl.num_programs(0) - 1))
    def _():
        _mem_kv_init(mem_ref, gmem_ref, wkv_ref, kbd_s, vbd_s)


def _ffn_kernel(x_ref, g_ref, w1_ref, w2_ref, gfin_ref, out_ref, *, final_norm):
    d_ff = w2_ref.shape[0]
    x = x_ref[...]
    h = _rmsnorm(x, g_ref[...]).astype(BF16)
    y = x
    for c0 in range(0, d_ff, FFN_CHUNK):
        gate = _dot(h, w1_ref[:, c0:c0 + FFN_CHUNK])
        up = _dot(h, w1_ref[:, d_ff + c0:d_ff + c0 + FFN_CHUNK])
        act = (gate * jax.nn.sigmoid(gate) * up).astype(BF16)
        y = y + _dot(act, w2_ref[c0:c0 + FFN_CHUNK, :])
    if final_norm:
        y = _rmsnorm(y, gfin_ref[...])
    out_ref[...] = y


def _const_spec(shape):
    nd = len(shape)
    return pl.BlockSpec(shape, lambda *_: (0,) * nd)


def _layer_spec(stacked, layer):
    nd = stacked.ndim - 1
    return pl.BlockSpec((None,) + stacked.shape[1:], lambda *_: (layer,) + (0,) * nd,
                        pipeline_mode=pl.Buffered(1))


def _skewed_specs(D, M, ts, n_tiles, total):
    def tile(t):
        return (t // n_tiles, t % n_tiles, 0)
    x_spec = pl.BlockSpec((None, ts, D), lambda i: tile(jnp.minimum(i, total - 1)))
    out_spec = pl.BlockSpec((None, ts, D), lambda i: tile(jnp.maximum(i - 1, 0)))
    mem_spec = pl.BlockSpec((None, M, D), lambda i: (jnp.minimum(i, total - 1) // n_tiles, 0, 0))
    return x_spec, out_spec, mem_spec


def _hgrn_layer(x, mem, gmix, gmem, wkv, win, lb, onorm, wout, layer_idx, mixer_idx):
    B, S, D = x.shape
    M = mem.shape[1]
    d_tok = lb.shape[1]
    d_mem = D - d_tok
    n_heads = d_tok // HG_EXPAND
    ts = min(HG_TILE, S)
    n_tiles = S // ts
    assert S % ts == 0 and ts % CHUNK == 0 and d_tok % HG_EXPAND == 0 and ts // CHUNK <= SUBLANES

    kidx = jnp.arange(DIAG * HG_EXPAND) // HG_EXPAND
    esel = (kidx[:, None] == (jnp.arange(LANES)[None, :] % DIAG)).astype(BF16)
    t = jnp.arange(CHUNK)[:, None]
    s = jnp.arange(CHUNK)[None, :]
    masks = [(((t // w) % 2 == 1) & (s // w == t // w - 1)) for w in LEVELS]
    masks.append((s // DIAG == t // DIAG) & (s % DIAG <= t % DIAG))
    masks.append(s <= t)
    masks = jnp.stack(masks).astype(F32)

    total = B * n_tiles
    x_spec, out_spec, mem_spec = _skewed_specs(D, M, ts, n_tiles, total)
    kernel = functools.partial(_hgrn_layer_kernel, layer_idx=layer_idx, n_tiles=n_tiles)
    return pl.pallas_call(
        kernel,
        grid=(total + 1,),
        in_specs=[x_spec, mem_spec, _const_spec((1, D)), _const_spec((1, D)), _layer_spec(wkv, layer_idx),
                  _layer_spec(win, mixer_idx), _const_spec(lb.shape), _const_spec((1, d_tok)),
                  _layer_spec(wout, layer_idx), _const_spec(esel.shape),
                  _const_spec(masks.shape)],
        out_specs=out_spec,
        out_shape=jax.ShapeDtypeStruct(x.shape, F32),
        scratch_shapes=[
            pltpu.VMEM((n_heads, HG_EXPAND, HG_EXPAND), F32),
            pltpu.VMEM((1, d_mem, MEM_HEADS * M), BF16),
            pltpu.VMEM((1, MEM_HEADS * M, d_mem), BF16),
            pltpu.VMEM((ts, win.shape[2]), F32),
            pltpu.VMEM((ts, d_tok), F32),
            pltpu.VMEM((ts, d_tok), F32),
            pltpu.VMEM((ts, d_tok), BF16),
            pltpu.VMEM((ts, d_tok), BF16),
            pltpu.VMEM((ts, d_tok), BF16),
            pltpu.VMEM((len(LEVELS), ts, d_tok), BF16),
            pltpu.VMEM((min(ts, PD_ROWS), n_heads * DIAG * HG_EXPAND), BF16),
            pltpu.VMEM((n_heads, ts, LANES), F32),
            pltpu.VMEM((SUBLANES, d_tok), F32),
            pltpu.VMEM((ts, d_tok), F32),
            pltpu.VMEM((ts, d_tok), F32),
            pltpu.VMEM((ts, d_mem), BF16),
            pltpu.VMEM((ts, D), F32),
        ],
        compiler_params=pltpu.CompilerParams(
            dimension_semantics=("arbitrary",), vmem_limit_bytes=VMEM_LIMIT),
        name="hgrn_layer",
    )(x, mem, gmix.reshape(1, D), gmem.reshape(1, D), wkv, win, lb, onorm.reshape(1, d_tok), wout,
      esel, masks)


def _gmlp_layer(x, mem, gmix, gmem, wkv, win, lng, lnb, ws, bs, wout, layer_idx, mixer_idx):
    B, S, D = x.shape
    M = mem.shape[1]
    d_tok = lng.shape[0]
    d_mem = D - d_tok
    n_groups = ws.shape[0]
    gdim = d_tok // n_groups
    ts = min(GM_TILE, S)
    n_tiles = S // ts
    assert S % ts == 0 and ts % GM_CHUNK == 0 and ws.shape[1:] == (GM_CHUNK, GM_CHUNK)
    bias = jnp.repeat(bs.astype(F32).T, gdim, axis=1)

    total = B * n_tiles
    x_spec, out_spec, mem_spec = _skewed_specs(D, M, ts, n_tiles, total)
    return pl.pallas_call(
        functools.partial(_gmlp_layer_kernel, n_tiles=n_tiles),
        grid=(total + 1,),
        in_specs=[x_spec, mem_spec, _const_spec((1, D)), _const_spec((1, D)), _layer_spec(wkv, layer_idx),
                  _layer_spec(win, mixer_idx), _const_spec((1, d_tok)), _const_spec((1, d_tok)),
                  _const_spec(ws.shape), _const_spec(bias.shape), _layer_spec(wout, layer_idx)],
        out_specs=out_spec,
        out_shape=jax.ShapeDtypeStruct(x.shape, F32),
        scratch_shapes=[
            pltpu.VMEM((GM_ATT_GROUPS, d_mem // GM_ATT_GROUPS, MEM_HEADS // GM_ATT_GROUPS * M), BF16),
            pltpu.VMEM((GM_ATT_GROUPS, MEM_HEADS // GM_ATT_GROUPS * M, d_mem // GM_ATT_GROUPS), BF16),
            pltpu.VMEM((2, ts, win.shape[2]), F32),
            pltpu.VMEM((ts, d_tok), BF16),
            pltpu.VMEM((ts, D), F32),
        ],
        compiler_params=pltpu.CompilerParams(
            dimension_semantics=("arbitrary",), vmem_limit_bytes=VMEM_LIMIT),
        name="gmlp_layer",
    )(x, mem, gmix.reshape(1, D), gmem.reshape(1, D), wkv, win, lng.reshape(1, d_tok),
      lnb.reshape(1, d_tok), ws, bias, wout)


def _ffn_layer(x, g, w1, w2, gfin, layer_idx, final_norm):
    B, S, D = x.shape
    T = B * S
    tm = min(FFN_TILE, T)
    assert T % tm == 0 and w2.shape[1] % FFN_CHUNK == 0
    x2 = x.reshape(T, D)
    row_spec = pl.BlockSpec((tm, D), lambda i: (i, 0))
    kernel = functools.partial(_ffn_kernel, final_norm=final_norm)
    out = pl.pallas_call(
        kernel,
        grid=(T // tm,),
        in_specs=[row_spec, _const_spec((1, D)), _layer_spec(w1, layer_idx), _layer_spec(w2, layer_idx),
                  _const_spec((1, D))],
        out_specs=row_spec,
        out_shape=jax.ShapeDtypeStruct((T, D), F32),
        compiler_params=pltpu.CompilerParams(
            dimension_semantics=("arbitrary",), vmem_limit_bytes=VMEM_LIMIT),
        name="ffn_layer",
    )(x2, g.reshape(1, D), w1, w2, gfin.reshape(1, D))
    return out.reshape(B, S, D)


def kernel(x, mem, mix_norm, mem_norm, w_mem_kv, w_out, hg_w_in, hg_lb, hg_onorm, gm_w_in, gm_ln_g,
           gm_ln_b, gm_ws, gm_bs, ffn_norm, w_ffn_in, w_ffn_out, final_norm):
    depth = mix_norm.shape[0]
    n_mixers = 2
    x = x.astype(F32)
    mem = mem.astype(F32)
    wkv, wout, hg_win, gm_win, w1, w2 = (w.astype(BF16) for w in (w_mem_kv, w_out, hg_w_in, gm_w_in,
                                                                  w_ffn_in, w_ffn_out))
    for i in range(depth):
        j = i // n_mixers
        if i % n_mixers == 0:
            x = _hgrn_layer(x, mem, mix_norm[i], mem_norm[i], wkv, hg_win, hg_lb.astype(F32), hg_onorm[j],
                            wout, layer_idx=i, mixer_idx=j)
        else:
            x = _gmlp_layer(x, mem, mix_norm[i], mem_norm[i], wkv, gm_win, gm_ln_g[j], gm_ln_b[j],
                            gm_ws[j], gm_bs[j], wout, layer_idx=i, mixer_idx=j)
        x = _ffn_layer(x, ffn_norm[i], w1, w2, final_norm, layer_idx=i, final_norm=(i == depth - 1))
    return x
```

```python
import functools

import jax
import jax.numpy as jnp
from jax import lax
from jax.experimental import pallas as pl
from jax.experimental.pallas import tpu as pltpu

F32 = jnp.float32
BF16 = jnp.bfloat16
EPS = 1e-6

LANES = 128
SUBLANES = 8
HG_EXPAND = 128
CHUNK = 64
LEVELS = (32, 16, 8)
DIAG = 8
DIRECT_MIN_B = -80.0
GM_CHUNK = 128
MEM_HEADS = 4
GM_ATT_GROUPS = 2
PD_ROWS = 256
HG_TILE = 512
GM_TILE = 512
FFN_TILE = 1024
FFN_CHUNK = 256
VMEM_LIMIT = 56 * 1024 * 1024

_NT = (((1,), (1,)), ((), ()))
_TN = (((0,), (0,)), ((), ()))


def _dot(a, b):
    return jnp.dot(a, b, preferred_element_type=F32)


def _rmsnorm(x, g):
    return x * lax.rsqrt(jnp.mean(x * x, axis=-1, keepdims=True) + EPS) * g


def _mem_kv_init(mem_ref, gmem_ref, wkv_ref, kbd_s, vbd_s):
    n_groups, gw, glen = kbd_s.shape
    d_mem = n_groups * gw
    hdim = d_mem // MEM_HEADS
    hp = gw // hdim
    mlen = glen // hp
    mn = _rmsnorm(mem_ref[...], gmem_ref[...])
    kv = _dot(mn.astype(BF16), wkv_ref[...])
    k = kv[:, :d_mem] * (hdim ** -0.5)
    v = kv[:, d_mem:]
    kt = k.T
    row_head = lax.broadcasted_iota(jnp.int32, (gw, mlen), 0) // hdim
    col_head = lax.broadcasted_iota(jnp.int32, (mlen, gw), 1) // hdim
    for g in range(n_groups):
        ktg = kt[g * gw:(g + 1) * gw, :]
        vg = v[:, g * gw:(g + 1) * gw]
        for h in range(hp):
            kbd_s[g, :, h * mlen:(h + 1) * mlen] = jnp.where(row_head == h, ktg, 0.0).astype(BF16)
            vbd_s[g, h * mlen:(h + 1) * mlen, :] = jnp.where(col_head == h, vg, 0.0).astype(BF16)


def _mem_attention(qm, kbd_s, vbd_s):
    n_groups, gw, glen = kbd_s.shape
    hp = MEM_HEADS // n_groups
    mlen = glen // hp
    outs = []
    for g in range(n_groups):
        s = _dot(qm[:, g * gw:(g + 1) * gw].astype(BF16), kbd_s[g])
        probs = []
        for h in range(hp):
            sh = s[:, h * mlen:(h + 1) * mlen]
            e = jnp.exp(sh - jnp.max(sh, axis=-1, keepdims=True))
            probs.append((e / jnp.sum(e, axis=-1, keepdims=True)).astype(BF16))
        outs.append(_dot(jnp.concatenate(probs, axis=1), vbd_s[g]))
    return outs[0] if n_groups == 1 else jnp.concatenate(outs, axis=1)


def _chunk_cumsum(a):
    rows, d = a.shape
    row = lax.broadcasted_iota(jnp.int32, (SUBLANES, d), 0)
    outs = []
    carry = None
    for i in range(rows // SUBLANES):
        blk = a[i * SUBLANES:(i + 1) * SUBLANES, :]
        shift = 1
        while shift < SUBLANES:
            blk = blk + jnp.where(row >= shift, pltpu.roll(blk, shift, axis=0), 0.0)
            shift *= 2
        if i % (CHUNK // SUBLANES) != 0:
            blk = blk + carry
        carry = jnp.broadcast_to(blk[SUBLANES - 1:SUBLANES, :], (SUBLANES, d))
        outs.append(blk)
    return jnp.concatenate(outs, axis=0)


def _hgrn_layer_kernel(x_ref, mem_ref, gmix_ref, gmem_ref, wkv_ref, win_ref, lb_ref, onorm_ref,
                       wout_ref, esel_ref, masks_ref, out_ref,
                       skv_s, kbd_s, vbd_s, p_s, b_s, kk_s, v_s, qe_s, kd_s, xw_s, pd_s, rd_s,
                       ebl_s, o_s, gm_s, mo_s, xprev_s, *, layer_idx, n_tiles):
    ts = x_ref.shape[0]
    d_tok = b_s.shape[1]
    n_heads = d_tok // HG_EXPAND
    n_chunks = ts // CHUNK
    i = pl.program_id(0)
    has_tile = i < pl.num_programs(0) - 1

    @pl.when(jnp.logical_and(i % n_tiles == 0, has_tile))
    def _():
        skv_s[...] = jnp.zeros_like(skv_s)
        _mem_kv_init(mem_ref, gmem_ref, wkv_ref, kbd_s, vbd_s)

    @pl.when(i == 0)
    def _():
        o_s[...] = jnp.zeros_like(o_s)
        gm_s[...] = jnp.zeros_like(gm_s)
        mo_s[...] = jnp.zeros_like(mo_s)
        xprev_s[...] = jnp.zeros_like(xprev_s)
        ebl_s[...] = jnp.zeros_like(ebl_s)

    x = x_ref[...]
    h = _rmsnorm(x, gmix_ref[...]).astype(BF16)
    q_cols, f_cols, i_cols, g_cols = (slice(k * d_tok, (k + 1) * d_tok) for k in range(4))
    m_cols = slice(4 * d_tok, win_ref.shape[1])
    for cols in (f_cols, m_cols, q_cols, g_cols, i_cols):
        p_s[:, cols] = _dot(h, win_ref[:, cols])

    heads = []
    for hd in range(n_heads):
        cols = slice(hd * HG_EXPAND, (hd + 1) * HG_EXPAND)
        oh = o_s[:, cols]
        on = oh * lax.rsqrt(jnp.mean(oh * oh, axis=-1, keepdims=True) + EPS)
        heads.append((on * gm_s[:, cols]).astype(BF16))
    base = xprev_s[...] + _dot(mo_s[...], wout_ref[d_tok:, :])
    out_ref[...] = base + _dot(jnp.concatenate(heads, axis=1), wout_ref[:d_tok, :])
    xprev_s[...] = x

    gate = p_s[:, 3 * d_tok:4 * d_tok]
    gm_s[...] = gate * jax.nn.sigmoid(gate) * onorm_ref[...]
    mo_s[...] = _mem_attention(p_s[:, 4 * d_tok:], kbd_s, vbd_s).astype(BF16)

    lbl = lb_ref[...]
    lbe = jnp.exp(lbl - jnp.max(lbl, axis=0, keepdims=True))
    lbv = jnp.sum(lbe[:layer_idx + 1], axis=0, keepdims=True) / jnp.sum(lbe, axis=0, keepdims=True)

    f = lbv + (1.0 - lbv) * jax.nn.sigmoid(p_s[:, d_tok:2 * d_tok])
    kk_s[...] = 1.0 - f
    b = _chunk_cumsum(jnp.log(f))
    b_s[...] = b
    qe_s[...] = (p_s[:, :d_tok] * jnp.exp(b)).astype(BF16)
    v_s[...] = p_s[:, 2 * d_tok:3 * d_tok].astype(BF16)

    for c in range(n_chunks):
        rows = slice(c * CHUNK, (c + 1) * CHUNK)
        bl = b_s[(c + 1) * CHUNK - 1:(c + 1) * CHUNK, :]
        kd_s[rows, :] = (kk_s[rows, :] * jnp.exp(bl - b_s[rows, :])).astype(BF16)
        ebl_s[c:c + 1, :] = jnp.exp(bl)

    direct_ok = jnp.min(b) >= DIRECT_MIN_B

    @pl.when(has_tile)
    def _():
        def mix(scores_fn):
            et = ebl_s[...].T
            for c in range(n_chunks):
                rows = slice(c * CHUNK, (c + 1) * CHUNK)
                for hd in range(n_heads):
                    cols = slice(hd * HG_EXPAND, (hd + 1) * HG_EXPAND)
                    a = scores_fn(hd, rows, cols).astype(BF16)
                    vc = v_s[rows, cols]
                    skv = skv_s[hd]
                    lhs = jnp.concatenate([qe_s[rows, cols], a], axis=1)
                    rhs = jnp.concatenate([skv.astype(BF16), vc], axis=0)
                    o_s[rows, cols] = _dot(lhs, rhs)
                    upd = lax.dot_general(kd_s[rows, cols], vc, _TN, preferred_element_type=F32)
                    ecol = jnp.broadcast_to(et[hd * HG_EXPAND:(hd + 1) * HG_EXPAND, c:c + 1],
                                            (HG_EXPAND, HG_EXPAND))
                    skv_s[hd] = skv * ecol + upd

        @pl.when(direct_ok)
        def _():
            kx_s = xw_s.at[0]
            kx_s[...] = (kk_s[...] * jnp.exp(-b_s[...])).astype(BF16)
            causal = masks_ref[len(LEVELS) + 1] != 0.0

            def scores(hd, rows, cols):
                s = lax.dot_general(qe_s[rows, cols], kx_s[rows, cols], _NT, preferred_element_type=F32)
                return jnp.where(causal, s, 0.0)

            mix(scores)

        @pl.when(jnp.logical_not(direct_ok))
        def _():
            for li, w in enumerate(LEVELS):
                for g0 in range(0, ts, 2 * w):
                    m = b_s[g0 + w:g0 + w + 1, :]
                    xk = kk_s[g0:g0 + w, :] * jnp.exp(m - b_s[g0:g0 + w, :])
                    xq = p_s[g0 + w:g0 + 2 * w, :d_tok] * jnp.exp(b_s[g0 + w:g0 + 2 * w, :] - m)
                    xw_s[li, g0:g0 + 2 * w, :] = jnp.concatenate([xk, xq], axis=0).astype(BF16)

            def pd_body(i, r_base):
                r0 = pl.multiple_of(i * 2 * DIAG, 2 * DIAG)
                g0 = r_base + r0
                q16 = p_s[pl.ds(g0, 2 * DIAG), :d_tok]
                b16 = b_s[pl.ds(g0, 2 * DIAG), :]
                for c in range(DIAG):
                    def partner(ref):
                        top = jnp.broadcast_to(ref[pl.ds(g0 + c, 1), :], (DIAG, d_tok))
                        bot = jnp.broadcast_to(ref[pl.ds(g0 + DIAG + c, 1), :], (DIAG, d_tok))
                        return jnp.concatenate([top, bot], axis=0)
                    val = q16 * partner(kk_s) * jnp.exp(jnp.minimum(b16 - partner(b_s), 0.0))
                    vb = val.astype(BF16)
                    for hd in range(n_heads):
                        col = (hd * DIAG + c) * HG_EXPAND
                        pd_s[pl.ds(r0, 2 * DIAG), col:col + HG_EXPAND] = vb[:, hd * HG_EXPAND:(hd + 1) * HG_EXPAND]
                return r_base

            kdiag = DIAG * HG_EXPAND
            pd_rows = pd_s.shape[0]
            for r_base in range(0, ts, pd_rows):
                lax.fori_loop(0, pd_rows // (2 * DIAG), pd_body, r_base)
                for hd in range(n_heads):
                    rd_s[hd, r_base:r_base + pd_rows, :] = _dot(pd_s[:, hd * kdiag:(hd + 1) * kdiag], esel_ref[...])

            def scores(hd, rows, cols):
                a = rd_s[hd, rows, :CHUNK] * masks_ref[len(LEVELS)]
                for li in range(len(LEVELS)):
                    xl = xw_s[li, rows, cols]
                    a = a + lax.dot_general(xl, xl, _NT, preferred_element_type=F32) * masks_ref[li]
                return a

            mix(scores)


def _gmlp_layer_kernel(x_ref, mem_ref, gmix_ref, gmem_ref, wkv_ref, win_ref, lng_ref, lnb_ref,
                       ws_ref, bias_ref, wout_ref, out_ref, kbd_s, vbd_s, p_s, vn_s, xprev_s, *, n_tiles):
    ts = x_ref.shape[0]
    d_tok = vn_s.shape[1]
    n_groups = ws_ref.shape[0]
    gdim = d_tok // n_groups
    pp = p_s.at[1]

    i = pl.program_id(0)

    @pl.when(i == 0)
    def _():
        pp[...] = jnp.zeros(pp.shape, F32)
        xprev_s[...] = jnp.zeros_like(xprev_s)
        kbd_s[...] = jnp.zeros_like(kbd_s)
        vbd_s[...] = jnp.zeros_like(vbd_s)

    x = x_ref[...]
    h = _rmsnorm(x, gmix_ref[...]).astype(BF16)
    p_s[0] = _dot(h, win_ref[...])

    pz = pp[:, :2 * d_tok]
    z = 0.5 * pz * (1.0 + lax.erf(pz * (2.0 ** -0.5)))
    u = z[:, :d_tok]
    v = z[:, d_tok:]
    mu = jnp.mean(v, axis=-1, keepdims=True)
    vc = v - mu
    var = jnp.mean(vc * vc, axis=-1, keepdims=True)
    vn_s[...] = (vc * lax.rsqrt(var + EPS) * lng_ref[...] + lnb_ref[...]).astype(BF16)

    tril = (lax.broadcasted_iota(jnp.int32, (GM_CHUNK, GM_CHUNK), 0)
            >= lax.broadcasted_iota(jnp.int32, (GM_CHUNK, GM_CHUNK), 1))
    heads = []
    for g in range(n_groups):
        cols = slice(g * gdim, (g + 1) * gdim)
        wg = jnp.where(tril, ws_ref[g], 0.0).astype(BF16)
        sv = [_dot(wg, vn_s[n * GM_CHUNK:(n + 1) * GM_CHUNK, cols]) + bias_ref[:, cols]
              for n in range(ts // GM_CHUNK)]
        heads.append((u[:, cols] * jnp.concatenate(sv, axis=0)).astype(BF16))
    heads.append(_mem_attention(pp[:, 2 * d_tok:], kbd_s, vbd_s).astype(BF16))
    out_ref[...] = xprev_s[...] + _dot(jnp.concatenate(heads, axis=1), wout_ref[...])
    xprev_s[...] = x
    pp[...] = p_s[0]

    @pl.when(jnp.logical_and(i % n_tiles == 0, i < pl.num_programs(0) - 1))
    def _():
        _mem_kv_init(mem_ref, gmem_ref, wkv_ref, kbd_s, vbd_s)


def _ffn_kernel(x_ref, g_ref, w1_ref, w2_ref, gfin_ref, out_ref, *, final_norm):
    d_ff = w2_ref.shape[0]
    x = x_ref[...]
    h = _rmsnorm(x, g_ref[...]).astype(BF16)
    y = x
    for c0 in range(0, d_ff, FFN_CHUNK):
        gate = _dot(h, w1_ref[:, c0:c0 + FFN_CHUNK])
        up = _dot(h, w1_ref[:, d_ff + c0:d_ff + c0 + FFN_CHUNK])
        act = (gate * jax.nn.sigmoid(gate) * up).astype(BF16)
        y = y + _dot(act, w2_ref[c0:c0 + FFN_CHUNK, :])
    if final_norm:
        y = _rmsnorm(y, gfin_ref[...])
    out_ref[...] = y


def _const_spec(shape):
    nd = len(shape)
    return pl.BlockSpec(shape, lambda *_: (0,) * nd)


def _layer_spec(stacked, layer):
    nd = stacked.ndim - 1
    return pl.BlockSpec((None,) + stacked.shape[1:], lambda *_: (layer,) + (0,) * nd,
                        pipeline_mode=pl.Buffered(1))


def _skewed_specs(D, M, ts, n_tiles, total):
    def tile(t):
        return (t // n_tiles, t % n_tiles, 0)
    x_spec = pl.BlockSpec((None, ts, D), lambda i: tile(jnp.minimum(i, total - 1)))
    out_spec = pl.BlockSpec((None, ts, D), lambda i: tile(jnp.maximum(i - 1, 0)))
    mem_spec = pl.BlockSpec((None, M, D), lambda i: (jnp.minimum(i, total - 1) // n_tiles, 0, 0))
    return x_spec, out_spec, mem_spec


def _hgrn_layer(x, mem, gmix, gmem, wkv, win, lb, onorm, wout, layer_idx, mixer_idx):
    B, S, D = x.shape
    M = mem.shape[1]
    d_tok = lb.shape[1]
    d_mem = D - d_tok
    n_heads = d_tok // HG_EXPAND
    ts = min(HG_TILE, S)
    n_tiles = S // ts
    assert S % ts == 0 and ts % CHUNK == 0 and d_tok % HG_EXPAND == 0 and ts // CHUNK <= SUBLANES

    kidx = jnp.arange(DIAG * HG_EXPAND) // HG_EXPAND
    esel = (kidx[:, None] == (jnp.arange(LANES)[None, :] % DIAG)).astype(BF16)
    t = jnp.arange(CHUNK)[:, None]
    s = jnp.arange(CHUNK)[None, :]
    masks = [(((t // w) % 2 == 1) & (s // w == t // w - 1)) for w in LEVELS]
    masks.append((s // DIAG == t // DIAG) & (s % DIAG <= t % DIAG))
    masks.append(s <= t)
    masks = jnp.stack(masks).astype(F32)

    total = B * n_tiles
    x_spec, out_spec, mem_spec = _skewed_specs(D, M, ts, n_tiles, total)
    kernel = functools.partial(_hgrn_layer_kernel, layer_idx=layer_idx, n_tiles=n_tiles)
    return pl.pallas_call(
        kernel,
        grid=(total + 1,),
        in_specs=[x_spec, mem_spec, _const_spec((1, D)), _const_spec((1, D)), _layer_spec(wkv, layer_idx),
                  _layer_spec(win, mixer_idx), _const_spec(lb.shape), _const_spec((1, d_tok)),
                  _layer_spec(wout, layer_idx), _const_spec(esel.shape),
                  _const_spec(masks.shape)],
        out_specs=out_spec,
        out_shape=jax.ShapeDtypeStruct(x.shape, F32),
        scratch_shapes=[
            pltpu.VMEM((n_heads, HG_EXPAND, HG_EXPAND), F32),
            pltpu.VMEM((1, d_mem, MEM_HEADS * M), BF16),
            pltpu.VMEM((1, MEM_HEADS * M, d_mem), BF16),
            pltpu.VMEM((ts, win.shape[2]), F32),
            pltpu.VMEM((ts, d_tok), F32),
            pltpu.VMEM((ts, d_tok), F32),
            pltpu.VMEM((ts, d_tok), BF16),
            pltpu.VMEM((ts, d_tok), BF16),
            pltpu.VMEM((ts, d_tok), BF16),
            pltpu.VMEM((len(LEVELS), ts, d_tok), BF16),
            pltpu.VMEM((min(ts, PD_ROWS), n_heads * DIAG * HG_EXPAND), BF16),
            pltpu.VMEM((n_heads, ts, LANES), F32),
            pltpu.VMEM((SUBLANES, d_tok), F32),
            pltpu.VMEM((ts, d_tok), F32),
            pltpu.VMEM((ts, d_tok), F32),
            pltpu.VMEM((ts, d_mem), BF16),
            pltpu.VMEM((ts, D), F32),
        ],
        compiler_params=pltpu.CompilerParams(
            dimension_semantics=("arbitrary",), vmem_limit_bytes=VMEM_LIMIT),
        name="hgrn_layer",
    )(x, mem, gmix.reshape(1, D), gmem.reshape(1, D), wkv, win, lb, onorm.reshape(1, d_tok), wout,
      esel, masks)


def _gmlp_layer(x, mem, gmix, gmem, wkv, win, lng, lnb, ws, bs, wout, layer_idx, mixer_idx):
    B, S, D = x.shape
    M = mem.shape[1]
    d_tok = lng.shape[0]
    d_mem = D - d_tok
    n_groups = ws.shape[0]
    gdim = d_tok // n_groups
    ts = min(GM_TILE, S)
    n_tiles = S // ts
    assert S % ts == 0 and ts % GM_CHUNK == 0 and ws.shape[1:] == (GM_CHUNK, GM_CHUNK)
    bias = jnp.repeat(bs.astype(F32).T, gdim, axis=1)

    total = B * n_tiles
    x_spec, out_spec, mem_spec = _skewed_specs(D, M, ts, n_tiles, total)
    return pl.pallas_call(
        functools.partial(_gmlp_layer_kernel, n_tiles=n_tiles),
        grid=(total + 1,),
        in_specs=[x_spec, mem_spec, _const_spec((1, D)), _const_spec((1, D)), _layer_spec(wkv, layer_idx),
                  _layer_spec(win, mixer_idx), _const_spec((1, d_tok)), _const_spec((1, d_tok)),
                  _const_spec(ws.shape), _const_spec(bias.shape), _layer_spec(wout, layer_idx)],
        out_specs=out_spec,
        out_shape=jax.ShapeDtypeStruct(x.shape, F32),
        scratch_shapes=[
            pltpu.VMEM((GM_ATT_GROUPS, d_mem // GM_ATT_GROUPS, MEM_HEADS // GM_ATT_GROUPS * M), BF16),
            pltpu.VMEM((GM_ATT_GROUPS, MEM_HEADS // GM_ATT_GROUPS * M, d_mem // GM_ATT_GROUPS), BF16),
            pltpu.VMEM((2, ts, win.shape[2]), F32),
            pltpu.VMEM((ts, d_tok), BF16),
            pltpu.VMEM((ts, D), F32),
        ],
        compiler_params=pltpu.CompilerParams(
            dimension_semantics=("arbitrary",), vmem_limit_bytes=VMEM_LIMIT),
        name="gmlp_layer",
    )(x, mem, gmix.reshape(1, D), gmem.reshape(1, D), wkv, win, lng.reshape(1, d_tok),
      lnb.reshape(1, d_tok), ws, bias, wout)


def _ffn_layer(x, g, w1, w2, gfin, layer_idx, final_norm):
    B, S, D = x.shape
    T = B * S
    tm = min(FFN_TILE, T)
    assert T % tm == 0 and w2.shape[1] % FFN_CHUNK == 0
    x2 = x.reshape(T, D)
    row_spec = pl.BlockSpec((tm, D), lambda i: (i, 0))
    kernel = functools.partial(_ffn_kernel, final_norm=final_norm)
    out = pl.pallas_call(
        kernel,
        grid=(T // tm,),
        in_specs=[row_spec, _const_spec((1, D)), _layer_spec(w1, layer_idx), _layer_spec(w2, layer_idx),
                  _const_spec((1, D))],
        out_specs=row_spec,
        out_shape=jax.ShapeDtypeStruct((T, D), F32),
        compiler_params=pltpu.CompilerParams(
            dimension_semantics=("arbitrary",), vmem_limit_bytes=VMEM_LIMIT),
        name="ffn_layer",
    )(x2, g.reshape(1, D), w1, w2, gfin.reshape(1, D))
    return out.reshape(B, S, D)


def kernel(x, mem, mix_norm, mem_norm, w_mem_kv, w_out, hg_w_in, hg_lb, hg_onorm, gm_w_in, gm_ln_g,
           gm_ln_b, gm_ws, gm_bs, ffn_norm, w_ffn_in, w_ffn_out, final_norm):
    depth = mix_norm.shape[0]
    n_mixers = 2
    x = x.astype(F32)
    mem = mem.astype(F32)
    wkv, wout, hg_win, gm_win, w1, w2 = (w.astype(BF16) for w in (w_mem_kv, w_out, hg_w_in, gm_w_in,
                                                                  w_ffn_in, w_ffn_out))
    for i in range(depth):
        j = i // n_mixers
        if i % n_mixers == 0:
            x = _hgrn_layer(x, mem, mix_norm[i], mem_norm[i], wkv, hg_win, hg_lb.astype(F32), hg_onorm[j],
                            wout, layer_idx=i, mixer_idx=j)
        else:
            x = _gmlp_layer(x, mem, mix_norm[i], mem_norm[i], wkv, gm_win, gm_ln_g[j], gm_ln_b[j],
                            gm_ws[j], gm_bs[j], wout, layer_idx=i, mixer_idx=j)
        x = _ffn_layer(x, ffn_norm[i], w1, w2, final_norm, layer_idx=i, final_norm=(i == depth - 1))
    return x
```

```python
import functools

import jax
import jax.numpy as jnp
from jax import lax
from jax.experimental import pallas as pl
from jax.experimental.pallas import tpu as pltpu

F32 = jnp.float32
BF16 = jnp.bfloat16
EPS = 1e-6

LANES = 128
SUBLANES = 8
HG_EXPAND = 128
CHUNK = 64
LEVELS = (32, 16, 8)
DIAG = 8
DIRECT_MIN_B = -80.0
GM_CHUNK = 128
MEM_HEADS = 4
GM_ATT_GROUPS = 2
PD_ROWS = 256
HG_TILE = 512
GM_TILE = 512
FFN_TILE = 1024
FFN_CHUNK = 256
VMEM_LIMIT = 56 * 1024 * 1024

_NT = (((1,), (1,)), ((), ()))
_TN = (((0,), (0,)), ((), ()))


def _dot(a, b):
    return jnp.dot(a, b, preferred_element_type=F32)


def _rmsnorm(x, g):
    return x * lax.rsqrt(jnp.mean(x * x, axis=-1, keepdims=True) + EPS) * g


def _mem_kv_init(mem_ref, gmem_ref, wkv_ref, kbd_s, vbd_s):
    n_groups, gw, glen = kbd_s.shape
    d_mem = n_groups * gw
    hdim = d_mem // MEM_HEADS
    hp = gw // hdim
    mlen = glen // hp
    mn = _rmsnorm(mem_ref[...], gmem_ref[...])
    kv = _dot(mn.astype(BF16), wkv_ref[...])
    k = kv[:, :d_mem] * (hdim ** -0.5)
    v = kv[:, d_mem:]
    kt = k.T
    row_head = lax.broadcasted_iota(jnp.int32, (gw, mlen), 0) // hdim
    col_head = lax.broadcasted_iota(jnp.int32, (mlen, gw), 1) // hdim
    for g in range(n_groups):
        ktg = kt[g * gw:(g + 1) * gw, :]
        vg = v[:, g * gw:(g + 1) * gw]
        for h in range(hp):
            kbd_s[g, :, h * mlen:(h + 1) * mlen] = jnp.where(row_head == h, ktg, 0.0).astype(BF16)
            vbd_s[g, h * mlen:(h + 1) * mlen, :] = jnp.where(col_head == h, vg, 0.0).astype(BF16)


def _mem_attention(qm, kbd_s, vbd_s):
    n_groups, gw, glen = kbd_s.shape
    hp = MEM_HEADS // n_groups
    mlen = glen // hp
    outs = []
    for g in range(n_groups):
        s = _dot(qm[:, g * gw:(g + 1) * gw].astype(BF16), kbd_s[g])
        probs = []
        for h in range(hp):
            sh = s[:, h * mlen:(h + 1) * mlen]
            e = jnp.exp(sh - jnp.max(sh, axis=-1, keepdims=True))
            probs.append((e / jnp.sum(e, axis=-1, keepdims=True)).astype(BF16))
        outs.append(_dot(jnp.concatenate(probs, axis=1), vbd_s[g]))
    return outs[0] if n_groups == 1 else jnp.concatenate(outs, axis=1)


def _chunk_cumsum(a):
    rows, d = a.shape
    row = lax.broadcasted_iota(jnp.int32, (SUBLANES, d), 0)
    outs = []
    carry = None
    for i in range(rows // SUBLANES):
        blk = a[i * SUBLANES:(i + 1) * SUBLANES, :]
        shift = 1
        while shift < SUBLANES:
            blk = blk + jnp.where(row >= shift, pltpu.roll(blk, shift, axis=0), 0.0)
            shift *= 2
        if i % (CHUNK // SUBLANES) != 0:
            blk = blk + carry
        carry = jnp.broadcast_to(blk[SUBLANES - 1:SUBLANES, :], (SUBLANES, d))
        outs.append(blk)
    return jnp.concatenate(outs, axis=0)


def _hgrn_layer_kernel(x_ref, mem_ref, gmix_ref, gmem_ref, wkv_ref, win_ref, lb_ref, onorm_ref,
                       wout_ref, esel_ref, masks_ref, out_ref,
                       skv_s, kbd_s, vbd_s, p_s, b_s, kk_s, v_s, qe_s, kd_s, xw_s, pd_s, rd_s,
                       ebl_s, o_s, gm_s, mo_s, xprev_s, *, layer_idx, n_tiles):
    ts = x_ref.shape[0]
    d_tok = b_s.shape[1]
    n_heads = d_tok // HG_EXPAND
    n_chunks = ts // CHUNK
    i = pl.program_id(0)
    has_tile = i < pl.num_programs(0) - 1

    @pl.when(jnp.logical_and(i % n_tiles == 0, has_tile))
    def _():
        skv_s[...] = jnp.zeros_like(skv_s)
        _mem_kv_init(mem_ref, gmem_ref, wkv_ref, kbd_s, vbd_s)

    @pl.when(i == 0)
    def _():
        o_s[...] = jnp.zeros_like(o_s)
        gm_s[...] = jnp.zeros_like(gm_s)
        mo_s[...] = jnp.zeros_like(mo_s)
        xprev_s[...] = jnp.zeros_like(xprev_s)
        ebl_s[...] = jnp.zeros_like(ebl_s)

    x = x_ref[...]
    h = _rmsnorm(x, gmix_ref[...]).astype(BF16)
    q_cols, f_cols, i_cols, g_cols = (slice(k * d_tok, (k + 1) * d_tok) for k in range(4))
    m_cols = slice(4 * d_tok, win_ref.shape[1])
    for cols in (f_cols, m_cols, q_cols, g_cols, i_cols):
        p_s[:, cols] = _dot(h, win_ref[:, cols])

    heads = []
    for hd in range(n_heads):
        cols = slice(hd * HG_EXPAND, (hd + 1) * HG_EXPAND)
        oh = o_s[:, cols]
        on = oh * lax.rsqrt(jnp.mean(oh * oh, axis=-1, keepdims=True) + EPS)
        heads.append((on * gm_s[:, cols]).astype(BF16))
    base = xprev_s[...] + _dot(mo_s[...], wout_ref[d_tok:, :])
    out_ref[...] = base + _dot(jnp.concatenate(heads, axis=1), wout_ref[:d_tok, :])
    xprev_s[...] = x

    gate = p_s[:, 3 * d_tok:4 * d_tok]
    gm_s[...] = gate * jax.nn.sigmoid(gate) * onorm_ref[...]
    mo_s[...] = _mem_attention(p_s[:, 4 * d_tok:], kbd_s, vbd_s).astype(BF16)

    lbl = lb_ref[...]
    lbe = jnp.exp(lbl - jnp.max(lbl, axis=0, keepdims=True))
    lbv = jnp.sum(lbe[:layer_idx + 1], axis=0, keepdims=True) / jnp.sum(lbe, axis=0, keepdims=True)

    f = lbv + (1.0 - lbv) * jax.nn.sigmoid(p_s[:, d_tok:2 * d_tok])
    kk_s[...] = 1.0 - f
    b = _chunk_cumsum(jnp.log(f))
    b_s[...] = b
    qe_s[...] = (p_s[:, :d_tok] * jnp.exp(b)).astype(BF16)
    v_s[...] = p_s[:, 2 * d_tok:3 * d_tok].astype(BF16)

    for c in range(n_chunks):
        rows = slice(c * CHUNK, (c + 1) * CHUNK)
        bl = b_s[(c + 1) * CHUNK - 1:(c + 1) * CHUNK, :]
        kd_s[rows, :] = (kk_s[rows, :] * jnp.exp(bl - b_s[rows, :])).astype(BF16)
        ebl_s[c:c + 1, :] = jnp.exp(bl)

    direct_ok = jnp.min(b) >= DIRECT_MIN_B

    @pl.when(has_tile)
    def _():
        def mix(scores_fn):
            et = ebl_s[...].T
            for c in range(n_chunks):
                rows = slice(c * CHUNK, (c + 1) * CHUNK)
                for hd in range(n_heads):
                    cols = slice(hd * HG_EXPAND, (hd + 1) * HG_EXPAND)
                    a = scores_fn(hd, rows, cols).astype(BF16)
                    vc = v_s[rows, cols]
                    skv = skv_s[hd]
                    lhs = jnp.concatenate([qe_s[rows, cols], a], axis=1)
                    rhs = jnp.concatenate([skv.astype(BF16), vc], axis=0)
                    o_s[rows, cols] = _dot(lhs, rhs)
                    upd = lax.dot_general(kd_s[rows, cols], vc, _TN, preferred_element_type=F32)
                    ecol = jnp.broadcast_to(et[hd * HG_EXPAND:(hd + 1) * HG_EXPAND, c:c + 1],
                                            (HG_EXPAND, HG_EXPAND))
                    skv_s[hd] = skv * ecol + upd

        @pl.when(direct_ok)
        def _():
            kx_s = xw_s.at[0]
            kx_s[...] = (kk_s[...] * jnp.exp(-b_s[...])).astype(BF16)
            causal = masks_ref[len(LEVELS) + 1] != 0.0

            def scores(hd, rows, cols):
                s = lax.dot_general(qe_s[rows, cols], kx_s[rows, cols], _NT, preferred_element_type=F32)
                return jnp.where(causal, s, 0.0)

            mix(scores)

        @pl.when(jnp.logical_not(direct_ok))
        def _():
            for li, w in enumerate(LEVELS):
                for g0 in range(0, ts, 2 * w):
                    m = b_s[g0 + w:g0 + w + 1, :]
                    xk = kk_s[g0:g0 + w, :] * jnp.exp(m - b_s[g0:g0 + w, :])
                    xq = p_s[g0 + w:g0 + 2 * w, :d_tok] * jnp.exp(b_s[g0 + w:g0 + 2 * w, :] - m)
                    xw_s[li, g0:g0 + 2 * w, :] = jnp.concatenate([xk, xq], axis=0).astype(BF16)

            def pd_body(i, r_base):
                r0 = pl.multiple_of(i * 2 * DIAG, 2 * DIAG)
                g0 = r_base + r0
                q16 = p_s[pl.ds(g0, 2 * DIAG), :d_tok]
                b16 = b_s[pl.ds(g0, 2 * DIAG), :]
                for c in range(DIAG):
                    def partner(ref):
                        top = jnp.broadcast_to(ref[pl.ds(g0 + c, 1), :], (DIAG, d_tok))
                        bot = jnp.broadcast_to(ref[pl.ds(g0 + DIAG + c, 1), :], (DIAG, d_tok))
                        return jnp.concatenate([top, bot], axis=0)
                    val = q16 * partner(kk_s) * jnp.exp(jnp.minimum(b16 - partner(b_s), 0.0))
                    vb = val.astype(BF16)
                    for hd in range(n_heads):
                        col = (hd * DIAG + c) * HG_EXPAND
                        pd_s[pl.ds(r0, 2 * DIAG), col:col + HG_EXPAND] = vb[:, hd * HG_EXPAND:(hd + 1) * HG_EXPAND]
                return r_base

            kdiag = DIAG * HG_EXPAND
            pd_rows = pd_s.shape[0]
            for r_base in range(0, ts, pd_rows):
                lax.fori_loop(0, pd_rows // (2 * DIAG), pd_body, r_base)
                for hd in range(n_heads):
                    rd_s[hd, r_base:r_base + pd_rows, :] = _dot(pd_s[:, hd * kdiag:(hd + 1) * kdiag], esel_ref[...])

            def scores(hd, rows, cols):
                a = rd_s[hd, rows, :CHUNK] * masks_ref[len(LEVELS)]
                for li in range(len(LEVELS)):
                    xl = xw_s[li, rows, cols]
                    a = a + lax.dot_general(xl, xl, _NT, preferred_element_type=F32) * masks_ref[li]
                return a

            mix(scores)


def _gmlp_layer_kernel(x_ref, mem_ref, gmix_ref, gmem_ref, wkv_ref, win_ref, lng_ref, lnb_ref,
                       ws_ref, bias_ref, wout_ref, out_ref, kbd_s, vbd_s, p_s, vn_s, xprev_s, *, n_tiles):
    ts = x_ref.shape[0]
    d_tok = vn_s.shape[1]
    n_groups = ws_ref.shape[0]
    gdim = d_tok // n_groups
    i = pl.program_id(0)

    @pl.when(i == 0)
    def _():
        p_s[1] = jnp.zeros(p_s.shape[1:], F32)
        xprev_s[...] = jnp.zeros_like(xprev_s)
        kbd_s[...] = jnp.zeros_like(kbd_s)
        vbd_s[...] = jnp.zeros_like(vbd_s)

    def step(cur, prev):
        x = x_ref[...]
        h = _rmsnorm(x, gmix_ref[...]).astype(BF16)
        p_s[cur] = _dot(h, win_ref[...])

        pp = p_s.at[prev]
        pz = pp[:, :2 * d_tok]
        z = 0.5 * pz * (1.0 + lax.erf(pz * (2.0 ** -0.5)))
        u = z[:, :d_tok]
        v = z[:, d_tok:]
        mu = jnp.mean(v, axis=-1, keepdims=True)
        vc = v - mu
        var = jnp.mean(vc * vc, axis=-1, keepdims=True)
        vn_s[...] = (vc * lax.rsqrt(var + EPS) * lng_ref[...] + lnb_ref[...]).astype(BF16)

        tril = (lax.broadcasted_iota(jnp.int32, (GM_CHUNK, GM_CHUNK), 0)
                >= lax.broadcasted_iota(jnp.int32, (GM_CHUNK, GM_CHUNK), 1))
        heads = []
        for g in range(n_groups):
            cols = slice(g * gdim, (g + 1) * gdim)
            wg = jnp.where(tril, ws_ref[g], 0.0).astype(BF16)
            sv = [_dot(wg, vn_s[n * GM_CHUNK:(n + 1) * GM_CHUNK, cols]) + bias_ref[:, cols]
                  for n in range(ts // GM_CHUNK)]
            heads.append((u[:, cols] * jnp.concatenate(sv, axis=0)).astype(BF16))
        heads.append(_mem_attention(pp[:, 2 * d_tok:], kbd_s, vbd_s).astype(BF16))
        out_ref[...] = xprev_s[...] + _dot(jnp.concatenate(heads, axis=1), wout_ref[...])
        xprev_s[...] = x

    @pl.when(i % 2 == 0)
    def _():
        step(0, 1)

    @pl.when(i % 2 == 1)
    def _():
        step(1, 0)

    @pl.when(jnp.logical_and(i % n_tiles == 0, i < pl.num_programs(0) - 1))
    def _():
        _mem_kv_init(mem_ref, gmem_ref, wkv_ref, kbd_s, vbd_s)


def _ffn_kernel(x_ref, g_ref, w1_ref, w2_ref, gfin_ref, out_ref, *, final_norm):
    d_ff = w2_ref.shape[0]
    x = x_ref[...]
    h = _rmsnorm(x, g_ref[...]).astype(BF16)
    y = x
    for c0 in range(0, d_ff, FFN_CHUNK):
        gate = _dot(h, w1_ref[:, c0:c0 + FFN_CHUNK])
        up = _dot(h, w1_ref[:, d_ff + c0:d_ff + c0 + FFN_CHUNK])
        act = (gate * jax.nn.sigmoid(gate) * up).astype(BF16)
        y = y + _dot(act, w2_ref[c0:c0 + FFN_CHUNK, :])
    if final_norm:
        y = _rmsnorm(y, gfin_ref[...])
    out_ref[...] = y


def _const_spec(shape):
    nd = len(shape)
    return pl.BlockSpec(shape, lambda *_: (0,) * nd)


def _layer_spec(stacked, layer):
    nd = stacked.ndim - 1
    return pl.BlockSpec((None,) + stacked.shape[1:], lambda *_: (layer,) + (0,) * nd,
                        pipeline_mode=pl.Buffered(1))


def _skewed_specs(D, M, ts, n_tiles, total):
    def tile(t):
        return (t // n_tiles, t % n_tiles, 0)
    x_spec = pl.BlockSpec((None, ts, D), lambda i: tile(jnp.minimum(i, total - 1)))
    out_spec = pl.BlockSpec((None, ts, D), lambda i: tile(jnp.maximum(i - 1, 0)))
    mem_spec = pl.BlockSpec((None, M, D), lambda i: (jnp.minimum(i, total - 1) // n_tiles, 0, 0))
    return x_spec, out_spec, mem_spec


def _hgrn_layer(x, mem, gmix, gmem, wkv, win, lb, onorm, wout, layer_idx, mixer_idx):
    B, S, D = x.shape
    M = mem.shape[1]
    d_tok = lb.shape[1]
    d_mem = D - d_tok
    n_heads = d_tok // HG_EXPAND
    ts = min(HG_TILE, S)
    n_tiles = S // ts
    assert S % ts == 0 and ts % CHUNK == 0 and d_tok % HG_EXPAND == 0 and ts // CHUNK <= SUBLANES

    kidx = jnp.arange(DIAG * HG_EXPAND) // HG_EXPAND
    esel = (kidx[:, None] == (jnp.arange(LANES)[None, :] % DIAG)).astype(BF16)
    t = jnp.arange(CHUNK)[:, None]
    s = jnp.arange(CHUNK)[None, :]
    masks = [(((t // w) % 2 == 1) & (s // w == t // w - 1)) for w in LEVELS]
    masks.append((s // DIAG == t // DIAG) & (s % DIAG <= t % DIAG))
    masks.append(s <= t)
    masks = jnp.stack(masks).astype(F32)

    total = B * n_tiles
    x_spec, out_spec, mem_spec = _skewed_specs(D, M, ts, n_tiles, total)
    kernel = functools.partial(_hgrn_layer_kernel, layer_idx=layer_idx, n_tiles=n_tiles)
    return pl.pallas_call(
        kernel,
        grid=(total + 1,),
        in_specs=[x_spec, mem_spec, _const_spec((1, D)), _const_spec((1, D)), _layer_spec(wkv, layer_idx),
                  _layer_spec(win, mixer_idx), _const_spec(lb.shape), _const_spec((1, d_tok)),
                  _layer_spec(wout, layer_idx), _const_spec(esel.shape),
                  _const_spec(masks.shape)],
        out_specs=out_spec,
        out_shape=jax.ShapeDtypeStruct(x.shape, F32),
        scratch_shapes=[
            pltpu.VMEM((n_heads, HG_EXPAND, HG_EXPAND), F32),
            pltpu.VMEM((1, d_mem, MEM_HEADS * M), BF16),
            pltpu.VMEM((1, MEM_HEADS * M, d_mem), BF16),
            pltpu.VMEM((ts, win.shape[2]), F32),
            pltpu.VMEM((ts, d_tok), F32),
            pltpu.VMEM((ts, d_tok), F32),
            pltpu.VMEM((ts, d_tok), BF16),
            pltpu.VMEM((ts, d_tok), BF16),
            pltpu.VMEM((ts, d_tok), BF16),
            pltpu.VMEM((len(LEVELS), ts, d_tok), BF16),
            pltpu.VMEM((min(ts, PD_ROWS), n_heads * DIAG * HG_EXPAND), BF16),
            pltpu.VMEM((n_heads, ts, LANES), F32),
            pltpu.VMEM((SUBLANES, d_tok), F32),
            pltpu.VMEM((ts, d_tok), F32),
            pltpu.VMEM((ts, d_tok), F32),
            pltpu.VMEM((ts, d_mem), BF16),
            pltpu.VMEM((ts, D), F32),
        ],
        compiler_params=pltpu.CompilerParams(
            dimension_semantics=("arbitrary",), vmem_limit_bytes=VMEM_LIMIT),
        name="hgrn_layer",
    )(x, mem, gmix.reshape(1, D), gmem.reshape(1, D), wkv, win, lb, onorm.reshape(1, d_tok), wout,
      esel, masks)


def _gmlp_layer(x, mem, gmix, gmem, wkv, win, lng, lnb, ws, bs, wout, layer_idx, mixer_idx):
    B, S, D = x.shape
    M = mem.shape[1]
    d_tok = lng.shape[0]
    d_mem = D - d_tok
    n_groups = ws.shape[0]
    gdim = d_tok // n_groups
    ts = min(GM_TILE, S)
    n_tiles = S // ts
    assert S % ts == 0 and ts % GM_CHUNK == 0 and ws.shape[1:] == (GM_CHUNK, GM_CHUNK)
    bias = jnp.repeat(bs.astype(F32).T, gdim, axis=1)

    total = B * n_tiles
    x_spec, out_spec, mem_spec = _skewed_specs(D, M, ts, n_tiles, total)
    return pl.pallas_call(
        functools.partial(_gmlp_layer_kernel, n_tiles=n_tiles),
        grid=(total + 1,),
        in_specs=[x_spec, mem_spec, _const_spec((1, D)), _const_spec((1, D)), _layer_spec(wkv, layer_idx),
                  _layer_spec(win, mixer_idx), _const_spec((1, d_tok)), _const_spec((1, d_tok)),
                  _const_spec(ws.shape), _const_spec(bias.shape), _layer_spec(wout, layer_idx)],
        out_specs=out_spec,
        out_shape=jax.ShapeDtypeStruct(x.shape, F32),
        scratch_shapes=[
            pltpu.VMEM((GM_ATT_GROUPS, d_mem // GM_ATT_GROUPS, MEM_HEADS // GM_ATT_GROUPS * M), BF16),
            pltpu.VMEM((GM_ATT_GROUPS, MEM_HEADS // GM_ATT_GROUPS * M, d_mem // GM_ATT_GROUPS), BF16),
            pltpu.VMEM((2, ts, win.shape[2]), F32),
            pltpu.VMEM((ts, d_tok), BF16),
            pltpu.VMEM((ts, D), F32),
        ],
        compiler_params=pltpu.CompilerParams(
            dimension_semantics=("arbitrary",), vmem_limit_bytes=VMEM_LIMIT),
        name="gmlp_layer",
    )(x, mem, gmix.reshape(1, D), gmem.reshape(1, D), wkv, win, lng.reshape(1, d_tok),
      lnb.reshape(1, d_tok), ws, bias, wout)


def _ffn_layer(x, g, w1, w2, gfin, layer_idx, final_norm):
    B, S, D = x.shape
    T = B * S
    tm = min(FFN_TILE, T)
    assert T % tm == 0 and w2.shape[1] % FFN_CHUNK == 0
    x2 = x.reshape(T, D)
    row_spec = pl.BlockSpec((tm, D), lambda i: (i, 0))
    kernel = functools.partial(_ffn_kernel, final_norm=final_norm)
    out = pl.pallas_call(
        kernel,
        grid=(T // tm,),
        in_specs=[row_spec, _const_spec((1, D)), _layer_spec(w1, layer_idx), _layer_spec(w2, layer_idx),
                  _const_spec((1, D))],
        out_specs=row_spec,
        out_shape=jax.ShapeDtypeStruct((T, D), F32),
        compiler_params=pltpu.CompilerParams(
            dimension_semantics=("arbitrary",), vmem_limit_bytes=VMEM_LIMIT),
        name="ffn_layer",
    )(x2, g.reshape(1, D), w1, w2, gfin.reshape(1, D))
    return out.reshape(B, S, D)


def kernel(x, mem, mix_norm, mem_norm, w_mem_kv, w_out, hg_w_in, hg_lb, hg_onorm, gm_w_in, gm_ln_g,
           gm_ln_b, gm_ws, gm_bs, ffn_norm, w_ffn_in, w_ffn_out, final_norm):
    depth = mix_norm.shape[0]
    n_mixers = 2
    x = x.astype(F32)
    mem = mem.astype(F32)
    wkv, wout, hg_win, gm_win, w1, w2 = (w.astype(BF16) for w in (w_mem_kv, w_out, hg_w_in, gm_w_in,
                                                                  w_ffn_in, w_ffn_out))
    for i in range(depth):
        j = i // n_mixers
        if i % n_mixers == 0:
            x = _hgrn_layer(x, mem, mix_norm[i], mem_norm[i], wkv, hg_win, hg_lb.astype(F32), hg_onorm[j],
                            wout, layer_idx=i, mixer_idx=j)
        else:
            x = _gmlp_layer(x, mem, mix_norm[i], mem_norm[i], wkv, gm_win, gm_ln_g[j], gm_ln_b[j],
                            gm_ws[j], gm_bs[j], wout, layer_idx=i, mixer_idx=j)
        x = _ffn_layer(x, ffn_norm[i], w1, w2, final_norm, layer_idx=i, final_norm=(i == depth - 1))
    return x
```

```python
import functools

import jax
import jax.numpy as jnp
from jax import lax
from jax.experimental import pallas as pl
from jax.experimental.pallas import tpu as pltpu

F32 = jnp.float32
BF16 = jnp.bfloat16
EPS = 1e-6

LANES = 128
SUBLANES = 8
HG_EXPAND = 128
CHUNK = 64
LEVELS = (32, 16, 8)
DIAG = 8
DIRECT_MIN_B = -80.0
GM_CHUNK = 128
MEM_HEADS = 4
GM_ATT_GROUPS = 2
PD_ROWS = 256
HG_TILE = 512
GM_TILE = 512
FFN_TILE = 1024
FFN_CHUNK = 256
VMEM_LIMIT = 56 * 1024 * 1024

_NT = (((1,), (1,)), ((), ()))
_TN = (((0,), (0,)), ((), ()))


def _dot(a, b):
    return jnp.dot(a, b, preferred_element_type=F32)


def _rmsnorm(x, g):
    return x * lax.rsqrt(jnp.mean(x * x, axis=-1, keepdims=True) + EPS) * g


def _mem_kv_init(mem_ref, gmem_ref, wkv_ref, kbd_s, vbd_s):
    n_groups, gw, glen = kbd_s.shape
    d_mem = n_groups * gw
    hdim = d_mem // MEM_HEADS
    hp = gw // hdim
    mlen = glen // hp
    mn = _rmsnorm(mem_ref[...], gmem_ref[...])
    kv = _dot(mn.astype(BF16), wkv_ref[...])
    k = kv[:, :d_mem] * (hdim ** -0.5)
    v = kv[:, d_mem:]
    kt = k.T
    row_head = lax.broadcasted_iota(jnp.int32, (gw, mlen), 0) // hdim
    col_head = lax.broadcasted_iota(jnp.int32, (mlen, gw), 1) // hdim
    for g in range(n_groups):
        ktg = kt[g * gw:(g + 1) * gw, :]
        vg = v[:, g * gw:(g + 1) * gw]
        for h in range(hp):
            kbd_s[g, :, h * mlen:(h + 1) * mlen] = jnp.where(row_head == h, ktg, 0.0).astype(BF16)
            vbd_s[g, h * mlen:(h + 1) * mlen, :] = jnp.where(col_head == h, vg, 0.0).astype(BF16)


def _mem_attention(qm, kbd_s, vbd_s):
    n_groups, gw, glen = kbd_s.shape
    hp = MEM_HEADS // n_groups
    mlen = glen // hp
    outs = []
    for g in range(n_groups):
        s = _dot(qm[:, g * gw:(g + 1) * gw].astype(BF16), kbd_s[g])
        probs = []
        for h in range(hp):
            sh = s[:, h * mlen:(h + 1) * mlen]
            e = jnp.exp(sh - jnp.max(sh, axis=-1, keepdims=True))
            probs.append((e / jnp.sum(e, axis=-1, keepdims=True)).astype(BF16))
        outs.append(_dot(jnp.concatenate(probs, axis=1), vbd_s[g]))
    return outs[0] if n_groups == 1 else jnp.concatenate(outs, axis=1)


def _chunk_cumsum(a):
    rows, d = a.shape
    row = lax.broadcasted_iota(jnp.int32, (SUBLANES, d), 0)
    outs = []
    carry = None
    for i in range(rows // SUBLANES):
        blk = a[i * SUBLANES:(i + 1) * SUBLANES, :]
        shift = 1
        while shift < SUBLANES:
            blk = blk + jnp.where(row >= shift, pltpu.roll(blk, shift, axis=0), 0.0)
            shift *= 2
        if i % (CHUNK // SUBLANES) != 0:
            blk = blk + carry
        carry = jnp.broadcast_to(blk[SUBLANES - 1:SUBLANES, :], (SUBLANES, d))
        outs.append(blk)
    return jnp.concatenate(outs, axis=0)


def _hgrn_layer_kernel(x_ref, mem_ref, gmix_ref, gmem_ref, wkv_ref, win_ref, lb_ref, onorm_ref,
                       wout_ref, esel_ref, masks_ref, out_ref,
                       skv_s, kbd_s, vbd_s, p_s, b_s, kk_s, v_s, qe_s, kd_s, xw_s, pd_s, rd_s,
                       ebl_s, o_s, gm_s, mo_s, xprev_s, *, layer_idx, n_tiles):
    ts = x_ref.shape[0]
    d_tok = b_s.shape[1]
    n_heads = d_tok // HG_EXPAND
    n_chunks = ts // CHUNK
    i = pl.program_id(0)
    has_tile = i < pl.num_programs(0) - 1

    @pl.when(jnp.logical_and(i % n_tiles == 0, has_tile))
    def _():
        skv_s[...] = jnp.zeros_like(skv_s)
        _mem_kv_init(mem_ref, gmem_ref, wkv_ref, kbd_s, vbd_s)

    @pl.when(i == 0)
    def _():
        o_s[...] = jnp.zeros_like(o_s)
        gm_s[...] = jnp.zeros_like(gm_s)
        mo_s[...] = jnp.zeros_like(mo_s)
        xprev_s[...] = jnp.zeros_like(xprev_s)
        ebl_s[...] = jnp.zeros_like(ebl_s)

    x = x_ref[...]
    h = _rmsnorm(x, gmix_ref[...]).astype(BF16)
    q_cols, f_cols, i_cols, g_cols = (slice(k * d_tok, (k + 1) * d_tok) for k in range(4))
    m_cols = slice(4 * d_tok, win_ref.shape[1])
    for cols in (f_cols, m_cols, q_cols, g_cols, i_cols):
        p_s[:, cols] = _dot(h, win_ref[:, cols])

    heads = []
    for hd in range(n_heads):
        cols = slice(hd * HG_EXPAND, (hd + 1) * HG_EXPAND)
        oh = o_s[:, cols]
        on = oh * lax.rsqrt(jnp.mean(oh * oh, axis=-1, keepdims=True) + EPS)
        heads.append((on * gm_s[:, cols]).astype(BF16))
    base = xprev_s[...] + _dot(mo_s[...], wout_ref[d_tok:, :])
    out_ref[...] = base + _dot(jnp.concatenate(heads, axis=1), wout_ref[:d_tok, :])
    xprev_s[...] = x

    gate = p_s[:, 3 * d_tok:4 * d_tok]
    gm_s[...] = gate * jax.nn.sigmoid(gate) * onorm_ref[...]
    mo_s[...] = _mem_attention(p_s[:, 4 * d_tok:], kbd_s, vbd_s).astype(BF16)

    lbl = lb_ref[...]
    lbe = jnp.exp(lbl - jnp.max(lbl, axis=0, keepdims=True))
    lbv = jnp.sum(lbe[:layer_idx + 1], axis=0, keepdims=True) / jnp.sum(lbe, axis=0, keepdims=True)

    f = lbv + (1.0 - lbv) * jax.nn.sigmoid(p_s[:, d_tok:2 * d_tok])
    kk_s[...] = 1.0 - f
    b = _chunk_cumsum(jnp.log(f))
    b_s[...] = b
    qe_s[...] = (p_s[:, :d_tok] * jnp.exp(b)).astype(BF16)
    v_s[...] = p_s[:, 2 * d_tok:3 * d_tok].astype(BF16)

    for c in range(n_chunks):
        rows = slice(c * CHUNK, (c + 1) * CHUNK)
        bl = b_s[(c + 1) * CHUNK - 1:(c + 1) * CHUNK, :]
        kd_s[rows, :] = (kk_s[rows, :] * jnp.exp(bl - b_s[rows, :])).astype(BF16)
        ebl_s[c:c + 1, :] = jnp.exp(bl)

    direct_ok = jnp.min(b) >= DIRECT_MIN_B

    @pl.when(has_tile)
    def _():
        def mix(scores_fn):
            et = ebl_s[...].T
            for c in range(n_chunks):
                rows = slice(c * CHUNK, (c + 1) * CHUNK)
                for hd in range(n_heads):
                    cols = slice(hd * HG_EXPAND, (hd + 1) * HG_EXPAND)
                    a = scores_fn(hd, rows, cols).astype(BF16)
                    vc = v_s[rows, cols]
                    skv = skv_s[hd]
                    lhs = jnp.concatenate([qe_s[rows, cols], a], axis=1)
                    rhs = jnp.concatenate([skv.astype(BF16), vc], axis=0)
                    o_s[rows, cols] = _dot(lhs, rhs)
                    upd = lax.dot_general(kd_s[rows, cols], vc, _TN, preferred_element_type=F32)
                    ecol = jnp.broadcast_to(et[hd * HG_EXPAND:(hd + 1) * HG_EXPAND, c:c + 1],
                                            (HG_EXPAND, HG_EXPAND))
                    skv_s[hd] = skv * ecol + upd

        @pl.when(direct_ok)
        def _():
            kx_s = xw_s.at[0]
            kx_s[...] = (kk_s[...] * jnp.exp(-b_s[...])).astype(BF16)
            causal = masks_ref[len(LEVELS) + 1] != 0.0

            def scores(hd, rows, cols):
                s = lax.dot_general(qe_s[rows, cols], kx_s[rows, cols], _NT, preferred_element_type=F32)
                return jnp.where(causal, s, 0.0)

            mix(scores)

        @pl.when(jnp.logical_not(direct_ok))
        def _():
            for li, w in enumerate(LEVELS):
                for g0 in range(0, ts, 2 * w):
                    m = b_s[g0 + w:g0 + w + 1, :]
                    xk = kk_s[g0:g0 + w, :] * jnp.exp(m - b_s[g0:g0 + w, :])
                    xq = p_s[g0 + w:g0 + 2 * w, :d_tok] * jnp.exp(b_s[g0 + w:g0 + 2 * w, :] - m)
                    xw_s[li, g0:g0 + 2 * w, :] = jnp.concatenate([xk, xq], axis=0).astype(BF16)

            def pd_body(i, r_base):
                r0 = pl.multiple_of(i * 2 * DIAG, 2 * DIAG)
                g0 = r_base + r0
                q16 = p_s[pl.ds(g0, 2 * DIAG), :d_tok]
                b16 = b_s[pl.ds(g0, 2 * DIAG), :]
                for c in range(DIAG):
                    def partner(ref):
                        top = jnp.broadcast_to(ref[pl.ds(g0 + c, 1), :], (DIAG, d_tok))
                        bot = jnp.broadcast_to(ref[pl.ds(g0 + DIAG + c, 1), :], (DIAG, d_tok))
                        return jnp.concatenate([top, bot], axis=0)
                    val = q16 * partner(kk_s) * jnp.exp(jnp.minimum(b16 - partner(b_s), 0.0))
                    vb = val.astype(BF16)
                    for hd in range(n_heads):
                        col = (hd * DIAG + c) * HG_EXPAND
                        pd_s[pl.ds(r0, 2 * DIAG), col:col + HG_EXPAND] = vb[:, hd * HG_EXPAND:(hd + 1) * HG_EXPAND]
                return r_base

            kdiag = DIAG * HG_EXPAND
            pd_rows = pd_s.shape[0]
            for r_base in range(0, ts, pd_rows):
                lax.fori_loop(0, pd_rows // (2 * DIAG), pd_body, r_base)
                for hd in range(n_heads):
                    rd_s[hd, r_base:r_base + pd_rows, :] = _dot(pd_s[:, hd * kdiag:(hd + 1) * kdiag], esel_ref[...])

            def scores(hd, rows, cols):
                a = rd_s[hd, rows, :CHUNK] * masks_ref[len(LEVELS)]
                for li in range(len(LEVELS)):
                    xl = xw_s[li, rows, cols]
                    a = a + lax.dot_general(xl, xl, _NT, preferred_element_type=F32) * masks_ref[li]
                return a

            mix(scores)


def _gmlp_layer_kernel(x_ref, mem_ref, gmix_ref, gmem_ref, wkv_ref, win_ref, lng_ref, lnb_ref,
                       ws_ref, bias_ref, wout_ref, out_ref, kbd_s, vbd_s, p_s, vn_s, xprev_s, *, n_tiles):
    ts = x_ref.shape[0]
    d_tok = vn_s.shape[1]
    n_groups = ws_ref.shape[0]
    gdim = d_tok // n_groups
    i = pl.program_id(0)

    @pl.when(i == 0)
    def _():
        p_s[1] = jnp.zeros(p_s.shape[1:], F32)
        xprev_s[...] = jnp.zeros_like(xprev_s)
        kbd_s[...] = jnp.zeros_like(kbd_s)
        vbd_s[...] = jnp.zeros_like(vbd_s)

    def step(cur, prev):
        x = x_ref[...]
        h = _rmsnorm(x, gmix_ref[...]).astype(BF16)
        for c0, c1 in ((d_tok, 2 * d_tok), (2 * d_tok, win_ref.shape[1]), (0, d_tok)):
            p_s[cur, :, c0:c1] = _dot(h, win_ref[:, c0:c1])

        pp = p_s.at[prev]
        pz = pp[:, :2 * d_tok]
        z = 0.5 * pz * (1.0 + lax.erf(pz * (2.0 ** -0.5)))
        u = z[:, :d_tok]
        v = z[:, d_tok:]
        mu = jnp.mean(v, axis=-1, keepdims=True)
        vc = v - mu
        var = jnp.mean(vc * vc, axis=-1, keepdims=True)
        vn_s[...] = (vc * lax.rsqrt(var + EPS) * lng_ref[...] + lnb_ref[...]).astype(BF16)

        tril = (lax.broadcasted_iota(jnp.int32, (GM_CHUNK, GM_CHUNK), 0)
                >= lax.broadcasted_iota(jnp.int32, (GM_CHUNK, GM_CHUNK), 1))
        heads = []
        for g in range(n_groups):
            cols = slice(g * gdim, (g + 1) * gdim)
            wg = jnp.where(tril, ws_ref[g], 0.0).astype(BF16)
            sv = [_dot(wg, vn_s[n * GM_CHUNK:(n + 1) * GM_CHUNK, cols]) + bias_ref[:, cols]
                  for n in range(ts // GM_CHUNK)]
            heads.append((u[:, cols] * jnp.concatenate(sv, axis=0)).astype(BF16))
        heads.append(_mem_attention(pp[:, 2 * d_tok:], kbd_s, vbd_s).astype(BF16))
        out_ref[...] = xprev_s[...] + _dot(jnp.concatenate(heads, axis=1), wout_ref[...])
        xprev_s[...] = x

    @pl.when(i % 2 == 0)
    def _():
        step(0, 1)

    @pl.when(i % 2 == 1)
    def _():
        step(1, 0)

    @pl.when(jnp.logical_and(i % n_tiles == 0, i < pl.num_programs(0) - 1))
    def _():
        _mem_kv_init(mem_ref, gmem_ref, wkv_ref, kbd_s, vbd_s)


def _ffn_kernel(x_ref, g_ref, w1_ref, w2_ref, gfin_ref, out_ref, *, final_norm):
    d_ff = w2_ref.shape[0]
    x = x_ref[...]
    h = _rmsnorm(x, g_ref[...]).astype(BF16)
    y = x
    for c0 in range(0, d_ff, FFN_CHUNK):
        gate = _dot(h, w1_ref[:, c0:c0 + FFN_CHUNK])
        up = _dot(h, w1_ref[:, d_ff + c0:d_ff + c0 + FFN_CHUNK])
        act = (gate * jax.nn.sigmoid(gate) * up).astype(BF16)
        y = y + _dot(act, w2_ref[c0:c0 + FFN_CHUNK, :])
    if final_norm:
        y = _rmsnorm(y, gfin_ref[...])
    out_ref[...] = y


def _const_spec(shape):
    nd = len(shape)
    return pl.BlockSpec(shape, lambda *_: (0,) * nd)


def _layer_spec(stacked, layer):
    nd = stacked.ndim - 1
    return pl.BlockSpec((None,) + stacked.shape[1:], lambda *_: (layer,) + (0,) * nd,
                        pipeline_mode=pl.Buffered(1))


def _skewed_specs(D, M, ts, n_tiles, total):
    def tile(t):
        return (t // n_tiles, t % n_tiles, 0)
    x_spec = pl.BlockSpec((None, ts, D), lambda i: tile(jnp.minimum(i, total - 1)))
    out_spec = pl.BlockSpec((None, ts, D), lambda i: tile(jnp.maximum(i - 1, 0)))
    mem_spec = pl.BlockSpec((None, M, D), lambda i: (jnp.minimum(i, total - 1) // n_tiles, 0, 0))
    return x_spec, out_spec, mem_spec


def _hgrn_layer(x, mem, gmix, gmem, wkv, win, lb, onorm, wout, layer_idx, mixer_idx):
    B, S, D = x.shape
    M = mem.shape[1]
    d_tok = lb.shape[1]
    d_mem = D - d_tok
    n_heads = d_tok // HG_EXPAND
    ts = min(HG_TILE, S)
    n_tiles = S // ts
    assert S % ts == 0 and ts % CHUNK == 0 and d_tok % HG_EXPAND == 0 and ts // CHUNK <= SUBLANES

    kidx = jnp.arange(DIAG * HG_EXPAND) // HG_EXPAND
    esel = (kidx[:, None] == (jnp.arange(LANES)[None, :] % DIAG)).astype(BF16)
    t = jnp.arange(CHUNK)[:, None]
    s = jnp.arange(CHUNK)[None, :]
    masks = [(((t // w) % 2 == 1) & (s // w == t // w - 1)) for w in LEVELS]
    masks.append((s // DIAG == t // DIAG) & (s % DIAG <= t % DIAG))
    masks.append(s <= t)
    masks = jnp.stack(masks).astype(F32)

    total = B * n_tiles
    x_spec, out_spec, mem_spec = _skewed_specs(D, M, ts, n_tiles, total)
    kernel = functools.partial(_hgrn_layer_kernel, layer_idx=layer_idx, n_tiles=n_tiles)
    return pl.pallas_call(
        kernel,
        grid=(total + 1,),
        in_specs=[x_spec, mem_spec, _const_spec((1, D)), _const_spec((1, D)), _layer_spec(wkv, layer_idx),
                  _layer_spec(win, mixer_idx), _const_spec(lb.shape), _const_spec((1, d_tok)),
                  _layer_spec(wout, layer_idx), _const_spec(esel.shape),
                  _const_spec(masks.shape)],
        out_specs=out_spec,
        out_shape=jax.ShapeDtypeStruct(x.shape, F32),
        scratch_shapes=[
            pltpu.VMEM((n_heads, HG_EXPAND, HG_EXPAND), F32),
            pltpu.VMEM((1, d_mem, MEM_HEADS * M), BF16),
            pltpu.VMEM((1, MEM_HEADS * M, d_mem), BF16),
            pltpu.VMEM((ts, win.shape[2]), F32),
            pltpu.VMEM((ts, d_tok), F32),
            pltpu.VMEM((ts, d_tok), F32),
            pltpu.VMEM((ts, d_tok), BF16),
            pltpu.VMEM((ts, d_tok), BF16),
            pltpu.VMEM((ts, d_tok), BF16),
            pltpu.VMEM((len(LEVELS), ts, d_tok), BF16),
            pltpu.VMEM((min(ts, PD_ROWS), n_heads * DIAG * HG_EXPAND), BF16),
            pltpu.VMEM((n_heads, ts, LANES), F32),
            pltpu.VMEM((SUBLANES, d_tok), F32),
            pltpu.VMEM((ts, d_tok), F32),
            pltpu.VMEM((ts, d_tok), F32),
            pltpu.VMEM((ts, d_mem), BF16),
            pltpu.VMEM((ts, D), F32),
        ],
        compiler_params=pltpu.CompilerParams(
            dimension_semantics=("arbitrary",), vmem_limit_bytes=VMEM_LIMIT),
        name="hgrn_layer",
    )(x, mem, gmix.reshape(1, D), gmem.reshape(1, D), wkv, win, lb, onorm.reshape(1, d_tok), wout,
      esel, masks)


def _gmlp_layer(x, mem, gmix, gmem, wkv, win, lng, lnb, ws, bs, wout, layer_idx, mixer_idx):
    B, S, D = x.shape
    M = mem.shape[1]
    d_tok = lng.shape[0]
    d_mem = D - d_tok
    n_groups = ws.shape[0]
    gdim = d_tok // n_groups
    ts = min(GM_TILE, S)
    n_tiles = S // ts
    assert S % ts == 0 and ts % GM_CHUNK == 0 and ws.shape[1:] == (GM_CHUNK, GM_CHUNK)
    bias = jnp.repeat(bs.astype(F32).T, gdim, axis=1)

    total = B * n_tiles
    x_spec, out_spec, mem_spec = _skewed_specs(D, M, ts, n_tiles, total)
    return pl.pallas_call(
        functools.partial(_gmlp_layer_kernel, n_tiles=n_tiles),
        grid=(total + 1,),
        in_specs=[x_spec, mem_spec, _const_spec((1, D)), _const_spec((1, D)), _layer_spec(wkv, layer_idx),
                  _layer_spec(win, mixer_idx), _const_spec((1, d_tok)), _const_spec((1, d_tok)),
                  _const_spec(ws.shape), _const_spec(bias.shape), _layer_spec(wout, layer_idx)],
        out_specs=out_spec,
        out_shape=jax.ShapeDtypeStruct(x.shape, F32),
        scratch_shapes=[
            pltpu.VMEM((GM_ATT_GROUPS, d_mem // GM_ATT_GROUPS, MEM_HEADS // GM_ATT_GROUPS * M), BF16),
            pltpu.VMEM((GM_ATT_GROUPS, MEM_HEADS // GM_ATT_GROUPS * M, d_mem // GM_ATT_GROUPS), BF16),
            pltpu.VMEM((2, ts, win.shape[2]), F32),
            pltpu.VMEM((ts, d_tok), BF16),
            pltpu.VMEM((ts, D), F32),
        ],
        compiler_params=pltpu.CompilerParams(
            dimension_semantics=("arbitrary",), vmem_limit_bytes=VMEM_LIMIT),
        name="gmlp_layer",
    )(x, mem, gmix.reshape(1, D), gmem.reshape(1, D), wkv, win, lng.reshape(1, d_tok),
      lnb.reshape(1, d_tok), ws, bias, wout)


def _ffn_layer(x, g, w1, w2, gfin, layer_idx, final_norm):
    B, S, D = x.shape
    T = B * S
    tm = min(FFN_TILE, T)
    assert T % tm == 0 and w2.shape[1] % FFN_CHUNK == 0
    x2 = x.reshape(T, D)
    row_spec = pl.BlockSpec((tm, D), lambda i: (i, 0))
    kernel = functools.partial(_ffn_kernel, final_norm=final_norm)
    out = pl.pallas_call(
        kernel,
        grid=(T // tm,),
        in_specs=[row_spec, _const_spec((1, D)), _layer_spec(w1, layer_idx), _layer_spec(w2, layer_idx),
                  _const_spec((1, D))],
        out_specs=row_spec,
        out_shape=jax.ShapeDtypeStruct((T, D), F32),
        compiler_params=pltpu.CompilerParams(
            dimension_semantics=("arbitrary",), vmem_limit_bytes=VMEM_LIMIT),
        name="ffn_layer",
    )(x2, g.reshape(1, D), w1, w2, gfin.reshape(1, D))
    return out.reshape(B, S, D)


def kernel(x, mem, mix_norm, mem_norm, w_mem_kv, w_out, hg_w_in, hg_lb, hg_onorm, gm_w_in, gm_ln_g,
           gm_ln_b, gm_ws, gm_bs, ffn_norm, w_ffn_in, w_ffn_out, final_norm):
    depth = mix_norm.shape[0]
    n_mixers = 2
    x = x.astype(F32)
    mem = mem.astype(F32)
    wkv, wout, hg_win, gm_win, w1, w2 = (w.astype(BF16) for w in (w_mem_kv, w_out, hg_w_in, gm_w_in,
                                                                  w_ffn_in, w_ffn_out))
    for i in range(depth):
        j = i // n_mixers
        if i % n_mixers == 0:
            x = _hgrn_layer(x, mem, mix_norm[i], mem_norm[i], wkv, hg_win, hg_lb.astype(F32), hg_onorm[j],
                            wout, layer_idx=i, mixer_idx=j)
        else:
            x = _gmlp_layer(x, mem, mix_norm[i], mem_norm[i], wkv, gm_win, gm_ln_g[j], gm_ln_b[j],
                            gm_ws[j], gm_bs[j], wout, layer_idx=i, mixer_idx=j)
        x = _ffn_layer(x, ffn_norm[i], w1, w2, final_norm, layer_idx=i, final_norm=(i == depth - 1))
    return x
```
